```python
import jax, jax.numpy as jnp
from jax import lax
import numpy as np

D_MODEL = 1024
BATCH = 4
SEQ = 4096
DEPTH = 2

D_MIX = D_MODEL
N_BRANCH = 3
LRU_HEADS = 4
LRU_BLOCK = D_MIX // LRU_HEADS
LRU_CONV_WIDTH = 4
LRU_C = 8.0
HGRN_HEADS = 8
HGRN_DK = D_MIX // HGRN_HEADS
HGRN_DV = D_MIX // HGRN_HEADS
HGRN_CHUNK = 64
GATE_CLIP = 30.0
CONF_WIDTH = 31
MEM_LEN = 256
XA_HEADS = 4
XA_HEAD_DIM = D_MODEL // XA_HEADS
D_FF = 4 * D_MODEL
EPS = 1e-6
IN_SPLIT_SIZES = (D_MIX, D_MIX,
                  D_MIX, D_MIX, D_MIX, D_MIX,
                  D_MIX, D_MIX,
                  N_BRANCH * D_MODEL)
N_IN = sum(IN_SPLIT_SIZES)

kernel_name = "hybrid_rglru_hgrn2_conformer_block"


def rmsnorm(x, g):
    xf = x.astype(jnp.float32)
    y = xf * lax.rsqrt(jnp.mean(xf * xf, axis=-1, keepdims=True) + EPS)
    return (y * g.astype(jnp.float32)).astype(x.dtype)


def layernorm(x, g, b):
    xf = x.astype(jnp.float32)
    mu = jnp.mean(xf, axis=-1, keepdims=True)
    xc = xf - mu
    var = jnp.mean(xc * xc, axis=-1, keepdims=True)
    y = xc * lax.rsqrt(var + EPS) * g.astype(jnp.float32) + b.astype(jnp.float32)
    return y.astype(x.dtype)


def causal_dw_conv(x, w, b):
    k, c = w.shape
    y = lax.conv_general_dilated(
        x, w[:, None, :].astype(x.dtype), window_strides=(1,),
        padding=((k - 1, 0),), dimension_numbers=("NWC", "WIO", "NWC"),
        feature_group_count=c)
    return y + b.astype(x.dtype)


def _lin_combine(left, right):
    a_l, b_l = left
    a_r, b_r = right
    return a_l * a_r, a_r * b_l + b_r


def rglru_branch(xb, yb, conv_w, conv_b, wx, bx, wa, ba, lam):
    bsz, s, _ = xb.shape
    xc = causal_dw_conv(xb, conv_w, conv_b)
    xh = xc.reshape(bsz, s, LRU_HEADS, LRU_BLOCK)
    gate_x = jax.nn.sigmoid(jnp.einsum("bshi,hij->bshj", xh, wx).reshape(bsz, s, D_MIX) + bx)
    gate_a = jax.nn.sigmoid(jnp.einsum("bshi,hij->bshj", xh, wa).reshape(bsz, s, D_MIX) + ba)
    log_a = -LRU_C * gate_a.astype(jnp.float32) * jax.nn.softplus(-lam.astype(jnp.float32))
    log_a = jnp.minimum(log_a, -1e-6)
    a = jnp.exp(log_a)
    mult = jnp.sqrt(-jnp.expm1(2.0 * log_a))
    u = xc.astype(jnp.float32) * gate_x.astype(jnp.float32) * mult
    _, h = lax.associative_scan(_lin_combine, (a, u), axis=1)
    return h.astype(xb.dtype) * jax.nn.gelu(yb)


def hgrn2_branch(q, f, i, g, lb, norm_g):
    bsz, s, _ = q.shape
    dtype = q.dtype
    n_chunks = s // HGRN_CHUNK
    lb = lb.astype(jnp.float32)
    qf = jax.nn.silu(q.astype(jnp.float32)) * (HGRN_DK ** -0.5)
    z = jnp.clip(f.astype(jnp.float32), -GATE_CLIP, GATE_CLIP)
    logf = jax.nn.log_sigmoid(z) + jnp.log1p(lb * jnp.exp(-z))
    logf = jnp.minimum(logf, 0.0)
    kf = -jnp.expm1(logf)
    vf = i.astype(jnp.float32)

    def to_chunks(t):
        return t.reshape(bsz, n_chunks, HGRN_CHUNK, HGRN_HEADS, -1).transpose(1, 0, 3, 2, 4)

    causal = jnp.tril(jnp.ones((HGRN_CHUNK, HGRN_CHUNK), dtype=bool))[:, :, None]

    def step(state, inp):
        qc, kc, vc, gc = inp
        bcum = jnp.cumsum(gc, axis=2)
        o_inter = jnp.einsum("bhcd,bhde->bhce", qc * jnp.exp(bcum), state)
        diff = bcum[:, :, :, None, :] - bcum[:, :, None, :, :]
        decay = jnp.where(causal, jnp.exp(jnp.where(causal, diff, 0.0)), 0.0)
        attn = jnp.einsum("bhid,bhijd,bhjd->bhij", qc, decay, kc)
        o = o_inter + jnp.einsum("bhij,bhje->bhie", attn, vc)
        b_last = bcum[:, :, -1:, :]
        state = (jnp.exp(b_last[:, :, 0, :])[..., None] * state
                 + jnp.einsum("bhcd,bhce->bhde", kc * jnp.exp(b_last - bcum), vc))
        return state, o

    s0 = jnp.zeros((bsz, HGRN_HEADS, HGRN_DK, HGRN_DV), jnp.float32)
    _, o = lax.scan(step, s0, (to_chunks(qf), to_chunks(kf), to_chunks(vf), to_chunks(logf)))
    o = o.transpose(1, 0, 3, 2, 4).reshape(bsz, s, HGRN_HEADS, HGRN_DV)
    gg = jax.nn.silu(g.astype(jnp.float32)).reshape(bsz, s, HGRN_HEADS, HGRN_DV)
    o = rmsnorm(o, norm_g) * gg
    return o.reshape(bsz, s, D_MIX).astype(dtype)


def conformer_conv_branch(va, vg, dw_w, dw_b, ln_g, ln_b):
    u = va * jax.nn.sigmoid(vg)
    u = causal_dw_conv(u, dw_w, dw_b)
    u = layernorm(u, ln_g, ln_b)
    return jax.nn.silu(u)


def cross_attention(h, mem_n, wq, wkv, wo):
    bsz, s, _ = h.shape
    m = mem_n.shape[1]
    q = (h @ wq).reshape(bsz, s, XA_HEADS, XA_HEAD_DIM)
    k, v = jnp.split(mem_n @ wkv, 2, axis=-1)
    k = k.reshape(bsz, m, XA_HEADS, XA_HEAD_DIM)
    v = v.reshape(bsz, m, XA_HEADS, XA_HEAD_DIM)
    sc = jnp.einsum("bshd,bmhd->bhsm", q, k).astype(jnp.float32) * (XA_HEAD_DIM ** -0.5)
    p = jax.nn.softmax(sc, axis=-1).astype(h.dtype)
    o = jnp.einsum("bhsm,bmhd->bshd", p, v).reshape(bsz, s, D_MODEL)
    return o @ wo


def setup_inputs(seed: int = 0) -> dict:
    key = jax.random.key(seed)
    ks = iter(jax.random.split(key, 40))
    L = DEPTH

    def nrm(shape, scale):
        return jax.random.normal(next(ks), shape, jnp.float32) * scale

    def gain(shape):
        return 1.0 + nrm(shape, 0.02)

    x = nrm((BATCH, SEQ, D_MODEL), 1.0)
    mem = nrm((BATCH, MEM_LEN, D_MODEL), 1.0)
    hgrn_lower_bounds = nrm((L, D_MIX), 1.0)
    norm_pre_mix = gain((L, D_MODEL))
    norm_post_mix = gain((L, D_MODEL))
    w_in = nrm((L, D_MODEL, N_IN), D_MODEL ** -0.5)
    lru_conv_w = nrm((L, LRU_CONV_WIDTH, D_MIX), LRU_CONV_WIDTH ** -0.5)
    lru_conv_b = nrm((L, D_MIX), 0.01)
    lru_wx = nrm((L, LRU_HEADS, LRU_BLOCK, LRU_BLOCK), LRU_BLOCK ** -0.5)
    lru_bx = nrm((L, D_MIX), 0.01)
    lru_wa = nrm((L, LRU_HEADS, LRU_BLOCK, LRU_BLOCK), LRU_BLOCK ** -0.5)
    lru_ba = nrm((L, D_MIX), 0.01)
    u = jax.random.uniform(next(ks), (L, D_MIX), jnp.float32, 0.9, 0.999)
    a0 = u ** (1.0 / LRU_C)
    lru_lambda = jnp.log(a0) - jnp.log1p(-a0)
    hgrn_norm_g = gain((L, HGRN_DV))
    conf_dw_w = nrm((L, CONF_WIDTH, D_MIX), CONF_WIDTH ** -0.5)
    conf_dw_b = nrm((L, D_MIX), 0.01)
    conf_ln_g = gain((L, D_MIX))
    conf_ln_b = nrm((L, D_MIX), 0.01)
    w_branch = nrm((L, N_BRANCH, D_MIX, D_MODEL), D_MIX ** -0.5)
    w_mix_out = nrm((L, D_MODEL, D_MODEL), D_MODEL ** -0.5)
    norm_pre_xa = gain((L, D_MODEL))
    norm_post_xa = gain((L, D_MODEL))
    mem_norm_g = gain((L, D_MODEL))
    xa_wq = nrm((L, D_MODEL, D_MODEL), D_MODEL ** -0.5)
    xa_wkv = nrm((L, D_MODEL, 2 * D_MODEL), D_MODEL ** -0.5)
    xa_wo = nrm((L, D_MODEL, D_MODEL), D_MODEL ** -0.5)
    norm_pre_mlp = gain((L, D_MODEL))
    norm_post_mlp = gain((L, D_MODEL))
    mlp_w1 = nrm((L, D_MODEL, D_FF), D_MODEL ** -0.5)
    mlp_w2 = nrm((L, D_FF, D_MODEL), D_FF ** -0.5)
    return {"x": x, "mem": mem, "hgrn_lower_bounds": hgrn_lower_bounds,
            "norm_pre_mix": norm_pre_mix, "norm_post_mix": norm_post_mix, "w_in": w_in,
            "lru_conv_w": lru_conv_w, "lru_conv_b": lru_conv_b, "lru_wx": lru_wx,
            "lru_bx": lru_bx, "lru_wa": lru_wa, "lru_ba": lru_ba, "lru_lambda": lru_lambda,
            "hgrn_norm_g": hgrn_norm_g, "conf_dw_w": conf_dw_w, "conf_dw_b": conf_dw_b,
            "conf_ln_g": conf_ln_g, "conf_ln_b": conf_ln_b, "w_branch": w_branch,
            "w_mix_out": w_mix_out, "norm_pre_xa": norm_pre_xa, "norm_post_xa": norm_post_xa,
            "mem_norm_g": mem_norm_g, "xa_wq": xa_wq, "xa_wkv": xa_wkv, "xa_wo": xa_wo,
            "norm_pre_mlp": norm_pre_mlp, "norm_post_mlp": norm_post_mlp,
            "mlp_w1": mlp_w1, "mlp_w2": mlp_w2}


def reference(x, mem, hgrn_lower_bounds, norm_pre_mix, norm_post_mix, w_in,
              lru_conv_w, lru_conv_b, lru_wx, lru_bx, lru_wa, lru_ba, lru_lambda,
              hgrn_norm_g, conf_dw_w, conf_dw_b, conf_ln_g, conf_ln_b, w_branch,
              w_mix_out, norm_pre_xa, norm_post_xa, mem_norm_g, xa_wq, xa_wkv, xa_wo,
              norm_pre_mlp, norm_post_mlp, mlp_w1, mlp_w2):
    bsz, s, _ = x.shape
    lb_soft = jax.nn.softmax(hgrn_lower_bounds.astype(jnp.float32), axis=0)
    lb_all = jnp.cumsum(lb_soft, axis=0) - lb_soft[0]
    split_idx = list(np.cumsum(IN_SPLIT_SIZES)[:-1])

    for l in range(DEPTH):
        h = rmsnorm(x, norm_pre_mix[l])
        z = h @ w_in[l]
        lru_x, lru_y, hq, hf, hi, hg, ca, cg, gate_logits = jnp.split(z, split_idx, axis=-1)
        y_lru = rglru_branch(lru_x, lru_y, lru_conv_w[l], lru_conv_b[l], lru_wx[l], lru_bx[l],
                             lru_wa[l], lru_ba[l], lru_lambda[l])
        y_hgrn = hgrn2_branch(hq, hf, hi, hg, lb_all[l], hgrn_norm_g[l])
        y_conf = conformer_conv_branch(ca, cg, conf_dw_w[l], conf_dw_b[l], conf_ln_g[l], conf_ln_b[l])
        branches = jnp.stack([y_lru, y_hgrn, y_conf], axis=2)
        proj = jnp.einsum("bsnw,nwd->bsnd", branches, w_branch[l])
        gates = jax.nn.sigmoid(gate_logits.reshape(bsz, s, N_BRANCH, D_MODEL))
        mixed = jnp.sum(gates * proj, axis=2) @ w_mix_out[l]
        x = x + rmsnorm(mixed, norm_post_mix[l])
        h = rmsnorm(x, norm_pre_xa[l])
        mem_n = rmsnorm(mem, mem_norm_g[l])
        xa = cross_attention(h, mem_n, xa_wq[l], xa_wkv[l], xa_wo[l])
        x = x + rmsnorm(xa, norm_post_xa[l])
        h = rmsnorm(x, norm_pre_mlp[l])
        ff = jnp.square(jax.nn.relu(h @ mlp_w1[l])) @ mlp_w2[l]
        x = x + rmsnorm(ff, norm_post_mlp[l])
    return x
```

```python
import functools

import jax
import jax.numpy as jnp
from jax import lax
from jax.experimental import pallas as pl
from jax.experimental.pallas import tpu as pltpu

F32 = jnp.float32
MXU_DTYPE = jnp.bfloat16

EPS = 1e-6
LRU_HEADS = 4
LRU_CONV_WIDTH = 4
LRU_C = 8.0
HGRN_HEADS = 8
GATE_CLIP = 30.0
CONF_WIDTH = 31
XA_HEADS = 4
N_BRANCH = 3

LANES = 128
SUBLANES = 8
VMEM_LIMIT_BYTES = 56 * 1024 * 1024

MIX_TILE = 256
HGRN_CHUNK = 128
XA_TILE = 512
MLP_TILE = 512
MLP_FF_CHUNK = 1024
CONF_PAD = 32
LRU_PAD = 8

COL_LRU_X, COL_LRU_Y, COL_HQ, COL_HF, COL_HI, COL_HG, COL_CA, COL_CG, COL_GATE0 = range(9)


def _dot(a, b):
    return jnp.dot(a, b, preferred_element_type=F32)


def _dot_nt(a, b):
    return lax.dot_general(a, b, (((1,), (1,)), ((), ())), preferred_element_type=F32)


def _dot_tn(a, b):
    return lax.dot_general(a, b, (((0,), (0,)), ((), ())), preferred_element_type=F32)


def _mx(a):
    return a.astype(MXU_DTYPE)


def _rms(x, g):
    return x * lax.rsqrt(jnp.mean(x * x, axis=-1, keepdims=True) + EPS) * g


def _sigmoid(x):
    return jax.nn.sigmoid(x)


def _softplus(y):
    return jnp.maximum(y, 0.0) + jnp.log1p(jnp.exp(-jnp.abs(y)))


def _gelu_tanh(x):
    c = 0.7978845608028654
    return 0.5 * x * (1.0 + jnp.tanh(c * (x + 0.044715 * (x * x * x))))


def _lru_kernel(x_ref, gpre_ref, wx_in_ref, wy_in_ref, wg_in_ref, cw_ref, cb_ref,
                wgx_ref, bgx_ref, wga_ref, bga_ref, lam_ref, wbr_ref,
                out_ref, ext_ref, h_ref):
    ts, d = x_ref.shape
    blk = d // LRU_HEADS

    @pl.when(pl.program_id(1) == 0)
    def _():
        ext_ref[0:LRU_PAD, :] = jnp.zeros((LRU_PAD, d), F32)
        h_ref[...] = jnp.zeros_like(h_ref)

    x = x_ref[...]
    h = _mx(_rms(x, gpre_ref[...]))
    xb = _dot(h, wx_in_ref[...])
    yb = _dot(h, wy_in_ref[...])

    ext_ref[LRU_PAD:LRU_PAD + ts, :] = xb
    xc = cb_ref[...] + cw_ref[LRU_CONV_WIDTH - 1:LRU_CONV_WIDTH, :] * xb
    for k in range(LRU_CONV_WIDTH - 1):
        off = LRU_PAD - (LRU_CONV_WIDTH - 1) + k
        xc = xc + cw_ref[k:k + 1, :] * ext_ref[pl.ds(off, ts), :]
    ext_ref[0:LRU_PAD, :] = ext_ref[ts:ts + LRU_PAD, :]

    xcm = _mx(xc)
    gx = jnp.concatenate(
        [_dot(xcm[:, i * blk:(i + 1) * blk], wgx_ref[i]) for i in range(LRU_HEADS)], axis=1)
    ga = jnp.concatenate(
        [_dot(xcm[:, i * blk:(i + 1) * blk], wga_ref[i]) for i in range(LRU_HEADS)], axis=1)
    gate_x = _sigmoid(gx + bgx_ref[...])
    gate_a = _sigmoid(ga + bga_ref[...])

    log_a = jnp.minimum(-LRU_C * gate_a * _softplus(-lam_ref[...]), -1e-6)
    a = jnp.exp(log_a)
    mult = jnp.sqrt(-jnp.tanh(log_a) * (a * a + 1.0))
    u = xc * gate_x * mult

    row = lax.broadcasted_iota(jnp.int32, (ts, d), 0)
    sh = 1
    while sh < ts:
        valid = row >= sh
        a_sh = pltpu.roll(a, sh, axis=0)
        u_sh = pltpu.roll(u, sh, axis=0)
        u = jnp.where(valid, a * u_sh + u, u)
        a = jnp.where(valid, a * a_sh, a)
        sh *= 2
    hs = u + a * h_ref[...]
    h_ref[...] = hs[ts - 1:ts, :]

    y = hs * _gelu_tanh(yb)
    proj = _dot(_mx(y), wbr_ref[...])
    gl = _dot(h, wg_in_ref[...])
    out_ref[...] = _sigmoid(gl) * proj


def _hgrn_chunk(q, k, v, gl, st, pair_xor, c_ref, b_ref):
    c, dk = q.shape
    row = lax.broadcasted_iota(jnp.int32, (c, dk), 0)

    cs = gl
    for sh in (1, 2, 4):
        cs = cs + jnp.where((row & (SUBLANES - 1)) >= sh, pltpu.roll(cs, sh, axis=0), 0.0)
    c_ref[...] = cs
    off = jnp.zeros((1, dk), F32)
    parts = []
    for g in range(c // SUBLANES):
        parts.append(cs[g * SUBLANES:(g + 1) * SUBLANES, :] + off)
        off = off + c_ref[g * SUBLANES + SUBLANES - 1:(g + 1) * SUBLANES, :]
    bcum = jnp.concatenate(parts, axis=0)
    b_ref[...] = bcum
    b_last = b_ref[c - 1:c, :]

    def block_pairs(qs, ks, s):
        a = _dot_nt(_mx(qs), _mx(ks))
        return a if s == c else jnp.where(pair_xor < s, a, 0.0)

    attn = None
    s = c
    while s >= 4:
        half = s // 2
        if s >= SUBLANES:
            ref_rows = jnp.concatenate(
                [jnp.broadcast_to(b_ref[blk * s + half - 1:blk * s + half, :], (s, dk))
                 for blk in range(c // s)], axis=0)
        else:
            lo = jnp.concatenate(
                [jnp.broadcast_to(b_ref[g * SUBLANES + 1:g * SUBLANES + 2, :], (SUBLANES, dk))
                 for g in range(c // SUBLANES)], axis=0)
            hi = jnp.concatenate(
                [jnp.broadcast_to(b_ref[g * SUBLANES + 5:g * SUBLANES + 6, :], (SUBLANES, dk))
                 for g in range(c // SUBLANES)], axis=0)
            ref_rows = jnp.where((row & (SUBLANES - 1)) < 4, lo, hi)
        decay = jnp.exp(-jnp.abs(bcum - ref_rows))
        second = (row & (s - 1)) >= half
        a = block_pairs(jnp.where(second, q * decay, 0.0), jnp.where(second, 0.0, k * decay), s)
        attn = a if attn is None else attn + a
        s = half
    odd = (row & 1) == 1
    attn = attn + block_pairs(jnp.where(odd, q * jnp.exp(gl), 0.0), jnp.where(odd, 0.0, k), 2)
    diag = jnp.sum(q * k, axis=-1, keepdims=True)
    vm = _mx(v)
    o = _dot_nt(_mx(q * jnp.exp(bcum)), _mx(st)) + _dot(_mx(attn), vm) + diag * v
    st_new = st * jnp.exp(b_last) + _dot_tn(vm, _mx(k * jnp.exp(b_last - bcum)))
    return o, st_new


def _hgrn_kernel(layer, x_ref, gpre_ref, wq_ref, wf_ref, wi_ref, wgo_ref, wg_in_ref, lbraw_ref,
                 ng_ref, wbr_ref, acc_ref, out_ref, st_ref, y_ref, c_ref, b_ref):
    ts, d = x_ref.shape
    dk = d // HGRN_HEADS
    chunk = min(HGRN_CHUNK, ts)

    @pl.when(pl.program_id(1) == 0)
    def _():
        st_ref[...] = jnp.zeros_like(st_ref)

    n_layers = lbraw_ref.shape[0]
    rows = [lbraw_ref[r:r + 1, :] for r in range(n_layers)]
    mx = functools.reduce(jnp.maximum, rows)
    es = [jnp.exp(r - mx) for r in rows]
    tot = functools.reduce(lambda p, t: p + t, es)
    soft = [e / tot for e in es]
    lb = functools.reduce(lambda p, t: p + t, soft[:layer + 1]) - soft[0]

    x = x_ref[...]
    h = _mx(_rms(x, gpre_ref[...]))
    q = _dot(h, wq_ref[...])
    f = _dot(h, wf_ref[...])
    v = _dot(h, wi_ref[...])
    go = _dot(h, wgo_ref[...])

    qf = q * _sigmoid(q) * (float(dk) ** -0.5)
    z = jnp.clip(f, -GATE_CLIP, GATE_CLIP)
    logf = jnp.minimum(jnp.log(lb + (1.0 - lb) * _sigmoid(z)), 0.0)
    kf = jnp.maximum((1.0 - lb) * _sigmoid(-z), 0.0)
    gg = go * _sigmoid(go)
    pair_xor = (lax.broadcasted_iota(jnp.int32, (chunk, chunk), 0)
                ^ lax.broadcasted_iota(jnp.int32, (chunk, chunk), 1))

    for ci in range(ts // chunk):
        r0 = ci * chunk
        for hd in range(HGRN_HEADS):
            c0 = hd * dk
            o, st_new = _hgrn_chunk(qf[r0:r0 + chunk, c0:c0 + dk], kf[r0:r0 + chunk, c0:c0 + dk],
                                    v[r0:r0 + chunk, c0:c0 + dk], logf[r0:r0 + chunk, c0:c0 + dk],
                                    st_ref[hd], pair_xor, c_ref, b_ref)
            st_ref[hd] = st_new
            on = o * lax.rsqrt(jnp.mean(o * o, axis=-1, keepdims=True) + EPS) * ng_ref[...]
            y_ref[r0:r0 + chunk, c0:c0 + dk] = on * gg[r0:r0 + chunk, c0:c0 + dk]

    proj = _dot(_mx(y_ref[...]), wbr_ref[...])
    gl = _dot(h, wg_in_ref[...])
    out_ref[...] = acc_ref[...] + _sigmoid(gl) * proj


def _conf_kernel(x_ref, gpre_ref, wa_ref, wgl_ref, wg_in_ref, dww_ref, dwb_ref, lng_ref, lnb_ref,
                 wbr_ref, acc_ref, wout_ref, gpost_ref, out_ref, ext_ref):
    ts, d = x_ref.shape

    @pl.when(pl.program_id(1) == 0)
    def _():
        ext_ref[0:CONF_PAD, :] = jnp.zeros((CONF_PAD, d), F32)

    x = x_ref[...]
    h = _mx(_rms(x, gpre_ref[...]))
    va = _dot(h, wa_ref[...])
    vg = _dot(h, wgl_ref[...])
    u = va * _sigmoid(vg)

    ext_ref[CONF_PAD:CONF_PAD + ts, :] = u
    y = dwb_ref[...] + dww_ref[CONF_WIDTH - 1:CONF_WIDTH, :] * u
    for k in range(CONF_WIDTH - 1):
        off = CONF_PAD - (CONF_WIDTH - 1) + k
        y = y + dww_ref[k:k + 1, :] * ext_ref[pl.ds(off, ts), :]
    ext_ref[0:CONF_PAD, :] = ext_ref[ts:ts + CONF_PAD, :]

    mu = jnp.mean(y, axis=-1, keepdims=True)
    yc = y - mu
    var = jnp.mean(yc * yc, axis=-1, keepdims=True)
    yn = yc * lax.rsqrt(var + EPS) * lng_ref[...] + lnb_ref[...]
    yo = yn * _sigmoid(yn)

    proj = _dot(_mx(yo), wbr_ref[...])
    gl = _dot(h, wg_in_ref[...])
    mixed = acc_ref[...] + _sigmoid(gl) * proj
    out = _dot(_mx(mixed), wout_ref[...])
    out_ref[...] = x + _rms(out, gpost_ref[...])


def _kv_kernel(mem_ref, g_ref, wk_ref, wv_ref, k_ref, v_ref):
    mn = _mx(_rms(mem_ref[...], g_ref[...]))
    k_ref[...] = _dot(mn, wk_ref[...]).astype(k_ref.dtype)
    v_ref[...] = _dot(mn, wv_ref[...]).astype(v_ref.dtype)


def _xa_kernel(x_ref, gpre_ref, wq_ref, k_ref, v_ref, wo_ref, gpost_ref, out_ref, o_ref):
    ts, d = x_ref.shape
    hd_dim = d // XA_HEADS
    x = x_ref[...]
    h = _mx(_rms(x, gpre_ref[...]))
    q = _mx(_dot(h, wq_ref[...]) * (float(hd_dim) ** -0.5))
    for hd in range(XA_HEADS):
        c0 = hd * hd_dim
        sc = _dot_nt(q[:, c0:c0 + hd_dim], k_ref[:, c0:c0 + hd_dim])
        p = jnp.exp(sc - jnp.max(sc, axis=-1, keepdims=True))
        denom = jnp.sum(p, axis=-1, keepdims=True)
        o_ref[:, c0:c0 + hd_dim] = _dot(_mx(p), v_ref[:, c0:c0 + hd_dim]) / denom
    xa = _dot(_mx(o_ref[...]), wo_ref[...])
    out_ref[...] = x + _rms(xa, gpost_ref[...])


def _mlp_kernel(x_ref, gpre_ref, w1_ref, w2_ref, gpost_ref, out_ref):
    ff = w1_ref.shape[1]
    fc = min(MLP_FF_CHUNK, ff)
    x = x_ref[...]
    h = _mx(_rms(x, gpre_ref[...]))
    acc = jnp.zeros(x.shape, F32)
    for c in range(ff // fc):
        a = jnp.maximum(_dot(h, w1_ref[:, c * fc:(c + 1) * fc]), 0.0)
        acc = acc + _dot(_mx(a * a), w2_ref[c * fc:(c + 1) * fc, :])
    out_ref[...] = x + _rms(acc, gpost_ref[...])


def _params(n_axes):
    return pltpu.CompilerParams(dimension_semantics=("arbitrary",) * n_axes,
                                vmem_limit_bytes=VMEM_LIMIT_BYTES)


def _resident(block_shape, index_map):
    return pl.BlockSpec(block_shape, index_map, pipeline_mode=pl.Buffered(1))


def _seq_tile(s, want):
    t = min(want, s)
    assert s % t == 0 and t % SUBLANES == 0
    return t


def _mixer(x, layer, p):
    b, s, d = x.shape
    ts = _seq_tile(s, MIX_TILE)
    assert ts % min(HGRN_CHUNK, ts) == 0 and d % (HGRN_HEADS * LANES) == 0
    grid = (b, s // ts)
    tok = pl.BlockSpec((None, ts, d), lambda bi, si: (bi, si, 0))
    tok_shape = jax.ShapeDtypeStruct((b, s, d), F32)

    def lay(arr):
        return _resident((None,) + arr.shape[1:], lambda bi, si: (layer,) + (0,) * (arr.ndim - 1))

    def win(col):
        return _resident((None, d, d), lambda bi, si: (layer, 0, col))

    def wbr(n):
        return _resident((None, None, d, d), lambda bi, si: (layer, n, 0, 0))

    acc = pl.pallas_call(
        _lru_kernel, grid=grid, out_shape=tok_shape,
        in_specs=[tok, lay(p["g_pre_mix"]), win(COL_LRU_X), win(COL_LRU_Y), win(COL_GATE0 + 0),
                  lay(p["lru_conv_w"]), lay(p["lru_conv_b"]), lay(p["lru_wx"]), lay(p["lru_bx"]),
                  lay(p["lru_wa"]), lay(p["lru_ba"]), lay(p["lru_lambda"]), wbr(0)],
        out_specs=tok,
        scratch_shapes=[pltpu.VMEM((ts + LRU_PAD, d), F32), pltpu.VMEM((1, d), F32)],
        compiler_params=_params(2), name="mix_lru",
    )(x, p["g_pre_mix"], p["w_in"], p["w_in"], p["w_in"], p["lru_conv_w"], p["lru_conv_b"],
      p["lru_wx"], p["lru_bx"], p["lru_wa"], p["lru_ba"], p["lru_lambda"], p["w_branch"])

    dk = d // HGRN_HEADS
    chunk = min(HGRN_CHUNK, ts)
    acc = pl.pallas_call(
        functools.partial(_hgrn_kernel, layer), grid=grid, out_shape=tok_shape,
        in_specs=[tok, lay(p["g_pre_mix"]), win(COL_HQ), win(COL_HF), win(COL_HI), win(COL_HG),
                  win(COL_GATE0 + 1),
                  _resident(p["hgrn_lb"].shape, lambda bi, si: (0, 0)),
                  lay(p["hgrn_norm_g"]), wbr(1), tok],
        out_specs=tok,
        scratch_shapes=[pltpu.VMEM((HGRN_HEADS, dk, dk), F32), pltpu.VMEM((ts, d), F32),
                        pltpu.VMEM((chunk, dk), F32), pltpu.VMEM((chunk, dk), F32)],
        compiler_params=_params(2), name="mix_hgrn",
    )(x, p["g_pre_mix"], p["w_in"], p["w_in"], p["w_in"], p["w_in"], p["w_in"], p["hgrn_lb"],
      p["hgrn_norm_g"], p["w_branch"], acc)

    return pl.pallas_call(
        _conf_kernel, grid=grid, out_shape=tok_shape,
        in_specs=[tok, lay(p["g_pre_mix"]), win(COL_CA), win(COL_CG), win(COL_GATE0 + 2),
                  lay(p["conf_dw_w"]), lay(p["conf_dw_b"]), lay(p["conf_ln_g"]), lay(p["conf_ln_b"]),
                  wbr(2), tok,
                  _resident((None, d, d), lambda bi, si: (layer, 0, 0)), lay(p["g_post_mix"])],
        out_specs=tok,
        scratch_shapes=[pltpu.VMEM((ts + CONF_PAD, d), F32)],
        compiler_params=_params(2), name="mix_conf",
    )(x, p["g_pre_mix"], p["w_in"], p["w_in"], p["w_in"], p["conf_dw_w"], p["conf_dw_b"],
      p["conf_ln_g"], p["conf_ln_b"], p["w_branch"], acc, p["w_mix_out"], p["g_post_mix"])


def _cross_attention(x, mem, layer, p):
    b, s, d = x.shape
    m = mem.shape[1]
    lay = lambda arr, nax: _resident((None,) + arr.shape[1:],
                                     lambda *g: (layer,) + (0,) * (arr.ndim - 1))
    mem_spec = pl.BlockSpec((None, m, d), lambda bi: (bi, 0, 0))
    k, v = pl.pallas_call(
        _kv_kernel, grid=(b,),
        out_shape=[jax.ShapeDtypeStruct((b, m, d), MXU_DTYPE)] * 2,
        in_specs=[mem_spec, lay(p["g_mem"], 1),
                  _resident((None, d, d), lambda bi: (layer, 0, 0)),
                  _resident((None, d, d), lambda bi: (layer, 0, 1))],
        out_specs=[mem_spec, mem_spec],
        compiler_params=_params(1), name="xa_kv",
    )(mem, p["g_mem"], p["xa_wkv"], p["xa_wkv"])

    ts = _seq_tile(s, XA_TILE)
    tok = pl.BlockSpec((None, ts, d), lambda bi, si: (bi, si, 0))
    kv_spec = pl.BlockSpec((None, m, d), lambda bi, si: (bi, 0, 0))
    wsq = _resident((None, d, d), lambda bi, si: (layer, 0, 0))
    return pl.pallas_call(
        _xa_kernel, grid=(b, s // ts), out_shape=jax.ShapeDtypeStruct((b, s, d), F32),
        in_specs=[tok, lay(p["g_pre_xa"], 2), wsq, kv_spec, kv_spec, wsq, lay(p["g_post_xa"], 2)],
        out_specs=tok,
        scratch_shapes=[pltpu.VMEM((ts, d), F32)],
        compiler_params=_params(2), name="xa_attn",
    )(x, p["g_pre_xa"], p["xa_wq"], k, v, p["xa_wo"], p["g_post_xa"])


def _mlp(x, layer, p):
    b, s, d = x.shape
    ff = p["mlp_w1"].shape[2]
    ts = _seq_tile(s, MLP_TILE)
    tok = pl.BlockSpec((None, ts, d), lambda bi, si: (bi, si, 0))
    lay = lambda arr: _resident((None,) + arr.shape[1:], lambda bi, si: (layer,) + (0,) * (arr.ndim - 1))
    return pl.pallas_call(
        _mlp_kernel, grid=(b, s // ts), out_shape=jax.ShapeDtypeStruct((b, s, d), F32),
        in_specs=[tok, lay(p["g_pre_mlp"]), lay(p["mlp_w1"]), lay(p["mlp_w2"]), lay(p["g_post_mlp"])],
        out_specs=tok,
        compiler_params=_params(2), name="mlp",
    )(x, p["g_pre_mlp"], p["mlp_w1"], p["mlp_w2"], p["g_post_mlp"])


def kernel(x, mem, hgrn_lower_bounds, norm_pre_mix, norm_post_mix, w_in, lru_conv_w, lru_conv_b, lru_wx, lru_bx, lru_wa, lru_ba, lru_lambda, hgrn_norm_g, conf_dw_w, conf_dw_b, conf_ln_g, conf_ln_b, w_branch, w_mix_out, norm_pre_xa, norm_post_xa, mem_norm_g, xa_wq, xa_wkv, xa_wo, norm_pre_mlp, norm_post_mlp, mlp_w1, mlp_w2):
    depth = w_in.shape[0]
    row = lambda a: a[:, None, :].astype(F32)
    p = dict(
        g_pre_mix=row(norm_pre_mix), g_post_mix=row(norm_post_mix), w_in=_mx(w_in),
        lru_conv_w=lru_conv_w.astype(F32), lru_conv_b=row(lru_conv_b),
        lru_wx=_mx(lru_wx), lru_bx=row(lru_bx), lru_wa=_mx(lru_wa), lru_ba=row(lru_ba),
        lru_lambda=row(lru_lambda), hgrn_lb=hgrn_lower_bounds.astype(F32),
        hgrn_norm_g=row(hgrn_norm_g), conf_dw_w=conf_dw_w.astype(F32), conf_dw_b=row(conf_dw_b),
        conf_ln_g=row(conf_ln_g), conf_ln_b=row(conf_ln_b), w_branch=_mx(w_branch),
        w_mix_out=_mx(w_mix_out), g_pre_xa=row(norm_pre_xa), g_post_xa=row(norm_post_xa),
        g_mem=row(mem_norm_g), xa_wq=_mx(xa_wq), xa_wkv=_mx(xa_wkv), xa_wo=_mx(xa_wo),
        g_pre_mlp=row(norm_pre_mlp), g_post_mlp=row(norm_post_mlp),
        mlp_w1=_mx(mlp_w1), mlp_w2=_mx(mlp_w2))
    for layer in range(depth):
        x = _mixer(x, layer, p)
        x = _cross_attention(x, mem, layer, p)
        x = _mlp(x, layer, p)
    return x
```

```python
import functools

import jax
import jax.numpy as jnp
from jax import lax
from jax.experimental import pallas as pl
from jax.experimental.pallas import tpu as pltpu

F32 = jnp.float32
MXU_DTYPE = jnp.bfloat16

EPS = 1e-6
LRU_HEADS = 4
LRU_CONV_WIDTH = 4
LRU_C = 8.0
HGRN_HEADS = 8
GATE_CLIP = 30.0
CONF_WIDTH = 31
XA_HEADS = 4
LOG2E = 1.4426950408889634

LANES = 128
SUBLANES = 8
VMEM_LIMIT_BYTES = 56 * 1024 * 1024

MIX_TILE = 256
HGRN_CHUNK = 128
XA_TILE = 512
MLP_TILE = 512
MLP_FF_CHUNK = 1024
CONF_PAD = 32
LRU_PAD = 8

COL_LRU_X, COL_LRU_Y, COL_HQ, COL_HF, COL_HI, COL_HG, COL_CA, COL_CG, COL_GATE0 = range(9)


def _dot(a, b):
    return jnp.dot(a, b, preferred_element_type=F32)


def _dot_nt(a, b):
    return lax.dot_general(a, b, (((1,), (1,)), ((), ())), preferred_element_type=F32)


def _dot_tn(a, b):
    return lax.dot_general(a, b, (((0,), (0,)), ((), ())), preferred_element_type=F32)


def _mx(a):
    return a.astype(MXU_DTYPE)


def _rms(x, g):
    return x * lax.rsqrt(jnp.mean(x * x, axis=-1, keepdims=True) + EPS) * g


def _sigmoid(x):
    return jax.nn.sigmoid(x)


def _softplus(y):
    return jnp.maximum(y, 0.0) + jnp.log1p(jnp.exp(-jnp.abs(y)))


def _gelu_tanh(x):
    c = 0.7978845608028654
    return 0.5 * x * (1.0 + jnp.tanh(c * (x + 0.044715 * (x * x * x))))


def _lru_kernel(x_ref, gpre_ref, wx_in_ref, wy_in_ref, wg_in_ref, cw_ref, cb_ref,
                wgx_ref, bgx_ref, wga_ref, bga_ref, lam_ref, wbr_ref,
                out_ref, ext_ref, h_ref):
    ts, d = x_ref.shape
    blk = d // LRU_HEADS

    @pl.when(pl.program_id(1) == 0)
    def _():
        ext_ref[0:LRU_PAD, :] = jnp.zeros((LRU_PAD, d), F32)
        h_ref[...] = jnp.zeros_like(h_ref)

    x = x_ref[...]
    h = _mx(_rms(x, gpre_ref[...]))
    xb = _dot(h, wx_in_ref[...])
    yb = _dot(h, wy_in_ref[...])
    gl = _dot(h, wg_in_ref[...])

    ext_ref[LRU_PAD:LRU_PAD + ts, :] = xb
    xc = cb_ref[...] + cw_ref[LRU_CONV_WIDTH - 1:LRU_CONV_WIDTH, :] * xb
    for k in range(LRU_CONV_WIDTH - 1):
        off = LRU_PAD - (LRU_CONV_WIDTH - 1) + k
        xc = xc + cw_ref[k:k + 1, :] * ext_ref[pl.ds(off, ts), :]
    ext_ref[0:LRU_PAD, :] = ext_ref[ts:ts + LRU_PAD, :]

    xcm = _mx(xc)
    gx = jnp.concatenate(
        [_dot(xcm[:, i * blk:(i + 1) * blk], wgx_ref[i]) for i in range(LRU_HEADS)], axis=1)
    ga = jnp.concatenate(
        [_dot(xcm[:, i * blk:(i + 1) * blk], wga_ref[i]) for i in range(LRU_HEADS)], axis=1)
    gate_x = _sigmoid(gx + bgx_ref[...])
    gate_a = _sigmoid(ga + bga_ref[...])

    log_a = jnp.minimum(-LRU_C * gate_a * _softplus(-lam_ref[...]), -1e-6)
    a = jnp.exp(log_a)
    m = -jnp.tanh(log_a) * (a * a + 1.0)
    u = xc * gate_x * (m * lax.rsqrt(m))

    row = lax.broadcasted_iota(jnp.int32, (ts, d), 0)
    sh = 1
    while sh < ts:
        valid = row >= sh
        a_sh = pltpu.roll(a, sh, axis=0)
        u_sh = pltpu.roll(u, sh, axis=0)
        u = jnp.where(valid, a * u_sh + u, u)
        a = jnp.where(valid, a * a_sh, a)
        sh *= 2
    hs = u + a * h_ref[...]
    h_ref[...] = hs[ts - 1:ts, :]

    y = hs * _gelu_tanh(yb)
    proj = _dot(_mx(y), wbr_ref[...])
    out_ref[...] = _sigmoid(gl) * proj


def _hgrn_constants(c, d):
    row = lax.broadcasted_iota(jnp.int32, (c, d), 0)
    sub = row & (SUBLANES - 1)
    pair_xor = (lax.broadcasted_iota(jnp.int32, (c, c), 0)
                ^ lax.broadcasted_iota(jnp.int32, (c, c), 1))
    cst = dict(scan=[sub >= sh for sh in (1, 2, 4)], lo4=sub < 4, odd=(row & 1) == 1,
               second={s: (row & (s - 1)) >= s // 2 for s in (4, 8)},
               same_block={}, sign={})
    s = c // 2
    while s >= 2:
        cst["same_block"][s] = pair_xor < s
        s //= 2
    for s in (4, 8):
        cst["sign"][s] = jnp.where(cst["second"][s], 1.0, -1.0).astype(F32)
    return cst


def _hgrn_chunk(q, k, v, gl, st_ref, cst, c_ref, b_ref, heads):
    c, d = q.shape
    dk = d // heads
    ngrp = c // SUBLANES
    hcols = [slice(hd * dk, (hd + 1) * dk) for hd in range(heads)]

    cs = gl
    for sh, m in zip((1, 2, 4), cst["scan"]):
        cs = cs + jnp.where(m, pltpu.roll(cs, sh, axis=0), 0.0)
    c_ref[...] = cs
    off = jnp.zeros((1, d), F32)
    parts = []
    for g in range(ngrp):
        parts.append((cs[g * SUBLANES:(g + 1) * SUBLANES, :] + off) * LOG2E)
        off = off + c_ref[g * SUBLANES + SUBLANES - 1:(g + 1) * SUBLANES, :]
    b2 = jnp.concatenate(parts, axis=0)
    b_ref[...] = b2
    b2_last = b_ref[c - 1:c, :]

    def grp(arr, g0, n):
        return arr[g0 * SUBLANES:(g0 + n) * SUBLANES, :]

    def ref_row(r):
        return b_ref[r:r + 1, :]

    attn = [None] * heads

    def add_level(qs, ks, s):
        qm, km = _mx(qs), _mx(ks)
        for hd in range(heads):
            a = _dot_nt(qm[:, hcols[hd]], km[:, hcols[hd]])
            attn[hd] = a if attn[hd] is None else jnp.where(cst["same_block"][s], a, attn[hd])

    s = c
    while s >= 2 * SUBLANES:
        half = s // 2
        hg = half // SUBLANES
        qs, ks = [], []
        for blk in range(c // s):
            g0 = blk * (s // SUBLANES)
            rr = ref_row(blk * s + half - 1)
            ks += [grp(k, g0, hg) * jnp.exp2(rr - grp(b2, g0, hg)), jnp.zeros((half, d), F32)]
            qs += [jnp.zeros((half, d), F32),
                   grp(q, g0 + hg, hg) * jnp.exp2(grp(b2, g0 + hg, hg) - rr)]
        add_level(jnp.concatenate(qs, axis=0), jnp.concatenate(ks, axis=0), s)
        s = half
    for s in (8, 4):
        if s == 8:
            rows = jnp.concatenate([jnp.broadcast_to(ref_row(g * SUBLANES + 3), (SUBLANES, d))
                                    for g in range(ngrp)], axis=0)
        else:
            lo = jnp.concatenate([jnp.broadcast_to(ref_row(g * SUBLANES + 1), (SUBLANES, d))
                                  for g in range(ngrp)], axis=0)
            hi = jnp.concatenate([jnp.broadcast_to(ref_row(g * SUBLANES + 5), (SUBLANES, d))
                                  for g in range(ngrp)], axis=0)
            rows = jnp.where(cst["lo4"], lo, hi)
        decay = jnp.exp2((b2 - rows) * cst["sign"][s])
        second = cst["second"][s]
        add_level(jnp.where(second, q * decay, 0.0), jnp.where(second, 0.0, k * decay), s)
    odd = cst["odd"]
    add_level(jnp.where(odd, q * jnp.exp2(gl * LOG2E), 0.0), jnp.where(odd, 0.0, k), 2)

    qk = q * k
    qe = _mx(q * jnp.exp2(b2))
    ke = _mx(k * jnp.exp2(b2_last - b2))
    vm = _mx(v)
    dec = jnp.exp2(b2_last)
    outs = []
    for hd in range(heads):
        hc = hcols[hd]
        st = st_ref[hd]
        diag = jnp.sum(qk[:, hc], axis=-1, keepdims=True)
        outs.append(_dot_nt(qe[:, hc], _mx(st)) + _dot(_mx(attn[hd]), vm[:, hc]) + diag * v[:, hc])
        st_ref[hd] = st * dec[:, hc] + _dot_tn(vm[:, hc], ke[:, hc])
    return outs


def _hgrn_kernel(layer, x_ref, gpre_ref, wq_ref, wf_ref, wi_ref, wgo_ref, wg_in_ref, lbraw_ref,
                 ng_ref, wbr_ref, acc_ref, out_ref, st_ref, y_ref, c_ref, b_ref):
    ts, d = x_ref.shape
    dk = d // HGRN_HEADS
    chunk = min(HGRN_CHUNK, ts)

    @pl.when(pl.program_id(1) == 0)
    def _():
        st_ref[...] = jnp.zeros_like(st_ref)

    n_layers = lbraw_ref.shape[0]
    rows = [lbraw_ref[r:r + 1, :] for r in range(n_layers)]
    mx = functools.reduce(jnp.maximum, rows)
    es = [jnp.exp(r - mx) for r in rows]
    tot = functools.reduce(lambda p, t: p + t, es)
    soft = [e / tot for e in es]
    lb = functools.reduce(lambda p, t: p + t, soft[:layer + 1]) - soft[0]

    x = x_ref[...]
    h = _mx(_rms(x, gpre_ref[...]))
    q = _dot(h, wq_ref[...])
    f = _dot(h, wf_ref[...])
    v = _dot(h, wi_ref[...])
    go = _dot(h, wgo_ref[...])
    gl = _dot(h, wg_in_ref[...])

    qf = q * _sigmoid(q) * (float(dk) ** -0.5)
    sg = _sigmoid(jnp.clip(f, -GATE_CLIP, GATE_CLIP))
    logf = jnp.minimum(jnp.log(lb + (1.0 - lb) * sg), 0.0)
    kf = jnp.maximum((1.0 - lb) * (1.0 - sg), 0.0)
    gg = go * _sigmoid(go)
    cst = _hgrn_constants(chunk, d)

    for ci in range(ts // chunk):
        r0 = ci * chunk
        outs = _hgrn_chunk(qf[r0:r0 + chunk, :], kf[r0:r0 + chunk, :], v[r0:r0 + chunk, :],
                           logf[r0:r0 + chunk, :], st_ref, cst, c_ref, b_ref, HGRN_HEADS)
        for hd, o in enumerate(outs):
            c0 = hd * dk
            on = o * lax.rsqrt(jnp.mean(o * o, axis=-1, keepdims=True) + EPS) * ng_ref[...]
            y_ref[r0:r0 + chunk, c0:c0 + dk] = on * gg[r0:r0 + chunk, c0:c0 + dk]

    proj = _dot(_mx(y_ref[...]), wbr_ref[...])
    out_ref[...] = acc_ref[...] + _sigmoid(gl) * proj


def _conf_kernel(x_ref, gpre_ref, wa_ref, wgl_ref, wg_in_ref, dww_ref, dwb_ref, lng_ref, lnb_ref,
                 wbr_ref, acc_ref, wout_ref, gpost_ref, out_ref, ext_ref, sh_ref):
    ts, d = x_ref.shape

    @pl.when(pl.program_id(1) == 0)
    def _():
        ext_ref[0:CONF_PAD, :] = jnp.zeros((CONF_PAD, d), F32)

    x = x_ref[...]
    h = _mx(_rms(x, gpre_ref[...]))
    va = _dot(h, wa_ref[...])
    vg = _dot(h, wgl_ref[...])
    gl = _dot(h, wg_in_ref[...])
    u = va * _sigmoid(vg)

    ext_ref[CONF_PAD:CONF_PAD + ts, :] = u
    n_sh = ts + CONF_PAD - SUBLANES
    for r in range(1, SUBLANES):
        sh_ref[r - 1] = ext_ref[pl.ds(SUBLANES - r, n_sh), :]
    y = dwb_ref[...] + dww_ref[CONF_WIDTH - 1:CONF_WIDTH, :] * u
    for k in range(CONF_WIDTH - 1):
        off = CONF_PAD - (CONF_WIDTH - 1) + k
        r = (-off) % SUBLANES
        if r == 0:
            tap = ext_ref[pl.ds(off, ts), :]
        else:
            tap = sh_ref[r - 1, pl.ds(off - (SUBLANES - r), ts), :]
        y = y + dww_ref[k:k + 1, :] * tap
    ext_ref[0:CONF_PAD, :] = ext_ref[ts:ts + CONF_PAD, :]

    mu = jnp.mean(y, axis=-1, keepdims=True)
    yc = y - mu
    var = jnp.mean(yc * yc, axis=-1, keepdims=True)
    yn = yc * lax.rsqrt(var + EPS) * lng_ref[...] + lnb_ref[...]
    yo = yn * _sigmoid(yn)

    proj = _dot(_mx(yo), wbr_ref[...])
    mixed = acc_ref[...] + _sigmoid(gl) * proj
    out = _dot(_mx(mixed), wout_ref[...])
    out_ref[...] = x + _rms(out, gpost_ref[...])


def _kv_kernel(mem_ref, g_ref, wk_ref, wv_ref, k_ref, v_ref):
    mn = _mx(_rms(mem_ref[...], g_ref[...]))
    k_ref[...] = _dot(mn, wk_ref[...]).astype(k_ref.dtype)
    v_ref[...] = _dot(mn, wv_ref[...]).astype(v_ref.dtype)


def _xa_kernel(x_ref, gpre_ref, wq_ref, k_ref, v_ref, wo_ref, gpost_ref, out_ref, o_ref):
    ts, d = x_ref.shape
    hd_dim = d // XA_HEADS
    x = x_ref[...]
    h = _mx(_rms(x, gpre_ref[...]))
    q = _mx(_dot(h, wq_ref[...]) * (float(hd_dim) ** -0.5))
    for hd in range(XA_HEADS):
        c0 = hd * hd_dim
        sc = _dot_nt(q[:, c0:c0 + hd_dim], k_ref[:, c0:c0 + hd_dim])
        p = jnp.exp(sc - jnp.max(sc, axis=-1, keepdims=True))
        denom = jnp.sum(p, axis=-1, keepdims=True)
        o_ref[:, c0:c0 + hd_dim] = _dot(_mx(p), v_ref[:, c0:c0 + hd_dim]) / denom
    xa = _dot(_mx(o_ref[...]), wo_ref[...])
    out_ref[...] = x + _rms(xa, gpost_ref[...])


def _mlp_kernel(x_ref, gpre_ref, w1_ref, w2_ref, gpost_ref, out_ref):
    ff = w1_ref.shape[1]
    fc = min(MLP_FF_CHUNK, ff)
    x = x_ref[...]
    h = _mx(_rms(x, gpre_ref[...]))
    acc = jnp.zeros(x.shape, F32)
    for c in range(ff // fc):
        a = jnp.maximum(_dot(h, w1_ref[:, c * fc:(c + 1) * fc]), 0.0)
        acc = acc + _dot(_mx(a * a), w2_ref[c * fc:(c + 1) * fc, :])
    out_ref[...] = x + _rms(acc, gpost_ref[...])


def _params(n_axes):
    return pltpu.CompilerParams(dimension_semantics=("arbitrary",) * n_axes,
                                vmem_limit_bytes=VMEM_LIMIT_BYTES)


def _resident(block_shape, index_map):
    return pl.BlockSpec(block_shape, index_map, pipeline_mode=pl.Buffered(1))


def _seq_tile(s, want):
    t = min(want, s)
    assert s % t == 0 and t % SUBLANES == 0
    return t


def _mixer(x, layer, p):
    b, s, d = x.shape
    ts = _seq_tile(s, MIX_TILE)
    chunk = min(HGRN_CHUNK, ts)
    assert ts % chunk == 0 and chunk >= 2 * SUBLANES and d % (HGRN_HEADS * LANES) == 0
    dk = d // HGRN_HEADS
    grid = (b, s // ts)
    tok = pl.BlockSpec((None, ts, d), lambda bi, si: (bi, si, 0))
    tok_shape = jax.ShapeDtypeStruct((b, s, d), F32)

    def lay(arr):
        return _resident((None,) + arr.shape[1:], lambda bi, si: (layer,) + (0,) * (arr.ndim - 1))

    def win(col):
        return _resident((None, d, d), lambda bi, si: (layer, 0, col))

    def wbr(n):
        return _resident((None, None, d, d), lambda bi, si: (layer, n, 0, 0))

    acc = pl.pallas_call(
        _lru_kernel, grid=grid, out_shape=tok_shape,
        in_specs=[tok, lay(p["g_pre_mix"]), win(COL_LRU_X), win(COL_LRU_Y), win(COL_GATE0 + 0),
                  lay(p["lru_conv_w"]), lay(p["lru_conv_b"]), lay(p["lru_wx"]), lay(p["lru_bx"]),
                  lay(p["lru_wa"]), lay(p["lru_ba"]), lay(p["lru_lambda"]), wbr(0)],
        out_specs=tok,
        scratch_shapes=[pltpu.VMEM((ts + LRU_PAD, d), F32), pltpu.VMEM((1, d), F32)],
        compiler_params=_params(2), name="mix_lru",
    )(x, p["g_pre_mix"], p["w_in"], p["w_in"], p["w_in"], p["lru_conv_w"], p["lru_conv_b"],
      p["lru_wx"], p["lru_bx"], p["lru_wa"], p["lru_ba"], p["lru_lambda"], p["w_branch"])

    acc = pl.pallas_call(
        functools.partial(_hgrn_kernel, layer), grid=grid, out_shape=tok_shape,
        in_specs=[tok, lay(p["g_pre_mix"]), win(COL_HQ), win(COL_HF), win(COL_HI), win(COL_HG),
                  win(COL_GATE0 + 1),
                  _resident(p["hgrn_lb"].shape, lambda bi, si: (0, 0)),
                  lay(p["hgrn_norm_g"]), wbr(1), tok],
        out_specs=tok,
        scratch_shapes=[pltpu.VMEM((HGRN_HEADS, dk, dk), F32), pltpu.VMEM((ts, d), F32),
                        pltpu.VMEM((chunk, d), F32), pltpu.VMEM((chunk, d), F32)],
        compiler_params=_params(2), name="mix_hgrn",
    )(x, p["g_pre_mix"], p["w_in"], p["w_in"], p["w_in"], p["w_in"], p["w_in"], p["hgrn_lb"],
      p["hgrn_norm_g"], p["w_branch"], acc)

    return pl.pallas_call(
        _conf_kernel, grid=grid, out_shape=tok_shape,
        in_specs=[tok, lay(p["g_pre_mix"]), win(COL_CA), win(COL_CG), win(COL_GATE0 + 2),
                  lay(p["conf_dw_w"]), lay(p["conf_dw_b"]), lay(p["conf_ln_g"]), lay(p["conf_ln_b"]),
                  wbr(2), tok,
                  _resident((None, d, d), lambda bi, si: (layer, 0, 0)), lay(p["g_post_mix"])],
        out_specs=tok,
        scratch_shapes=[pltpu.VMEM((ts + CONF_PAD, d), F32),
                        pltpu.VMEM((SUBLANES - 1, ts + CONF_PAD - SUBLANES, d), F32)],
        compiler_params=_params(2), name="mix_conf",
    )(x, p["g_pre_mix"], p["w_in"], p["w_in"], p["w_in"], p["conf_dw_w"], p["conf_dw_b"],
      p["conf_ln_g"], p["conf_ln_b"], p["w_branch"], acc, p["w_mix_out"], p["g_post_mix"])


def _cross_attention(x, mem, layer, p):
    b, s, d = x.shape
    m = mem.shape[1]
    lay = lambda arr: _resident((None,) + arr.shape[1:], lambda *g: (layer,) + (0,) * (arr.ndim - 1))
    mem_spec = pl.BlockSpec((None, m, d), lambda bi: (bi, 0, 0))
    k, v = pl.pallas_call(
        _kv_kernel, grid=(b,),
        out_shape=[jax.ShapeDtypeStruct((b, m, d), MXU_DTYPE)] * 2,
        in_specs=[mem_spec, lay(p["g_mem"]),
                  _resident((None, d, d), lambda bi: (layer, 0, 0)),
                  _resident((None, d, d), lambda bi: (layer, 0, 1))],
        out_specs=[mem_spec, mem_spec],
        compiler_params=_params(1), name="xa_kv",
    )(mem, p["g_mem"], p["xa_wkv"], p["xa_wkv"])

    ts = _seq_tile(s, XA_TILE)
    tok = pl.BlockSpec((None, ts, d), lambda bi, si: (bi, si, 0))
    kv_spec = pl.BlockSpec((None, m, d), lambda bi, si: (bi, 0, 0))
    wsq = _resident((None, d, d), lambda bi, si: (layer, 0, 0))
    return pl.pallas_call(
        _xa_kernel, grid=(b, s // ts), out_shape=jax.ShapeDtypeStruct((b, s, d), F32),
        in_specs=[tok, lay(p["g_pre_xa"]), wsq, kv_spec, kv_spec, wsq, lay(p["g_post_xa"])],
        out_specs=tok,
        scratch_shapes=[pltpu.VMEM((ts, d), F32)],
        compiler_params=_params(2), name="xa_attn",
    )(x, p["g_pre_xa"], p["xa_wq"], k, v, p["xa_wo"], p["g_post_xa"])


def _mlp(x, layer, p):
    b, s, d = x.shape
    ts = _seq_tile(s, MLP_TILE)
    tok = pl.BlockSpec((None, ts, d), lambda bi, si: (bi, si, 0))
    lay = lambda arr: _resident((None,) + arr.shape[1:], lambda bi, si: (layer,) + (0,) * (arr.ndim - 1))
    return pl.pallas_call(
        _mlp_kernel, grid=(b, s // ts), out_shape=jax.ShapeDtypeStruct((b, s, d), F32),
        in_specs=[tok, lay(p["g_pre_mlp"]), lay(p["mlp_w1"]), lay(p["mlp_w2"]), lay(p["g_post_mlp"])],
        out_specs=tok,
        compiler_params=_params(2), name="mlp",
    )(x, p["g_pre_mlp"], p["mlp_w1"], p["mlp_w2"], p["g_post_mlp"])


def kernel(x, mem, hgrn_lower_bounds, norm_pre_mix, norm_post_mix, w_in, lru_conv_w, lru_conv_b, lru_wx, lru_bx, lru_wa, lru_ba, lru_lambda, hgrn_norm_g, conf_dw_w, conf_dw_b, conf_ln_g, conf_ln_b, w_branch, w_mix_out, norm_pre_xa, norm_post_xa, mem_norm_g, xa_wq, xa_wkv, xa_wo, norm_pre_mlp, norm_post_mlp, mlp_w1, mlp_w2):
    depth = w_in.shape[0]
    row = lambda a: a[:, None, :].astype(F32)
    p = dict(
        g_pre_mix=row(norm_pre_mix), g_post_mix=row(norm_post_mix), w_in=_mx(w_in),
        lru_conv_w=lru_conv_w.astype(F32), lru_conv_b=row(lru_conv_b),
        lru_wx=_mx(lru_wx), lru_bx=row(lru_bx), lru_wa=_mx(lru_wa), lru_ba=row(lru_ba),
        lru_lambda=row(lru_lambda), hgrn_lb=hgrn_lower_bounds.astype(F32),
        hgrn_norm_g=row(hgrn_norm_g), conf_dw_w=conf_dw_w.astype(F32), conf_dw_b=row(conf_dw_b),
        conf_ln_g=row(conf_ln_g), conf_ln_b=row(conf_ln_b), w_branch=_mx(w_branch),
        w_mix_out=_mx(w_mix_out), g_pre_xa=row(norm_pre_xa), g_post_xa=row(norm_post_xa),
        g_mem=row(mem_norm_g), xa_wq=_mx(xa_wq), xa_wkv=_mx(xa_wkv), xa_wo=_mx(xa_wo),
        g_pre_mlp=row(norm_pre_mlp), g_post_mlp=row(norm_post_mlp),
        mlp_w1=_mx(mlp_w1), mlp_w2=_mx(mlp_w2))
    for layer in range(depth):
        x = _mixer(x, layer, p)
        x = _cross_attention(x, mem, layer, p)
        x = _mlp(x, layer, p)
    return x
```

```python
import functools

import jax
import jax.numpy as jnp
from jax import lax
from jax.experimental import pallas as pl
from jax.experimental.pallas import tpu as pltpu

F32 = jnp.float32
MXU_DTYPE = jnp.bfloat16

EPS = 1e-6
LRU_HEADS = 4
LRU_CONV_WIDTH = 4
LRU_C = 8.0
HGRN_HEADS = 8
GATE_CLIP = 30.0
CONF_WIDTH = 31
XA_HEADS = 4
LOG2E = 1.4426950408889634
CONF_COLS = 256

LANES = 128
SUBLANES = 8
VMEM_LIMIT_BYTES = 56 * 1024 * 1024

MIX_TILE = 256
HGRN_CHUNK = 128
CONF_PAD = 32
LRU_PAD = 8

COL_LRU_X, COL_LRU_Y, COL_HQ, COL_HF, COL_HI, COL_HG, COL_CA, COL_CG, COL_GATE0 = range(9)


def _dot(a, b):
    return jnp.dot(a, b, preferred_element_type=F32)


def _dot_nt(a, b):
    return lax.dot_general(a, b, (((1,), (1,)), ((), ())), preferred_element_type=F32)


def _dot_tn(a, b):
    return lax.dot_general(a, b, (((0,), (0,)), ((), ())), preferred_element_type=F32)


def _mx(a):
    return a.astype(MXU_DTYPE)


def _rms(x, g):
    return x * lax.rsqrt(jnp.mean(x * x, axis=-1, keepdims=True) + EPS) * g


def _sigmoid(x):
    return jax.nn.sigmoid(x)


def _softplus(y):
    return jnp.maximum(y, 0.0) + jnp.log1p(jnp.exp(-jnp.abs(y)))


def _gelu_tanh(x):
    c = 0.7978845608028654
    return 0.5 * x * (1.0 + jnp.tanh(c * (x + 0.044715 * (x * x * x))))


def _lru_kernel(x_ref, gpre_ref, wx_in_ref, wy_in_ref, wg_in_ref, cw_ref, cb_ref,
                wgx_ref, bgx_ref, wga_ref, bga_ref, lam_ref, wbr_ref,
                out_ref, ext_ref, h_ref):
    ts, d = x_ref.shape
    blk = d // LRU_HEADS

    @pl.when(pl.program_id(1) == 0)
    def _():
        ext_ref[0:LRU_PAD, :] = jnp.zeros((LRU_PAD, d), F32)
        h_ref[...] = jnp.zeros_like(h_ref)

    x = x_ref[...]
    h = _mx(_rms(x, gpre_ref[...]))
    xb = _dot(h, wx_in_ref[...])
    yb = _dot(h, wy_in_ref[...])
    gl = _dot(h, wg_in_ref[...])

    ext_ref[LRU_PAD:LRU_PAD + ts, :] = xb
    xc = cb_ref[...] + cw_ref[LRU_CONV_WIDTH - 1:LRU_CONV_WIDTH, :] * xb
    for k in range(LRU_CONV_WIDTH - 1):
        off = LRU_PAD - (LRU_CONV_WIDTH - 1) + k
        xc = xc + cw_ref[k:k + 1, :] * ext_ref[pl.ds(off, ts), :]
    ext_ref[0:LRU_PAD, :] = ext_ref[ts:ts + LRU_PAD, :]

    xcm = _mx(xc)
    gx = jnp.concatenate(
        [_dot(xcm[:, i * blk:(i + 1) * blk], wgx_ref[i]) for i in range(LRU_HEADS)], axis=1)
    ga = jnp.concatenate(
        [_dot(xcm[:, i * blk:(i + 1) * blk], wga_ref[i]) for i in range(LRU_HEADS)], axis=1)
    gate_x = _sigmoid(gx + bgx_ref[...])
    gate_a = _sigmoid(ga + bga_ref[...])

    log_a = jnp.minimum(-LRU_C * gate_a * _softplus(-lam_ref[...]), -1e-6)
    a = jnp.exp(log_a)
    m = -jnp.tanh(log_a) * (a * a + 1.0)
    u = xc * gate_x * (m * lax.rsqrt(m))

    row = lax.broadcasted_iota(jnp.int32, (ts, d), 0)
    sh = 1
    while sh < ts:
        valid = row >= sh
        a_sh = pltpu.roll(a, sh, axis=0)
        u_sh = pltpu.roll(u, sh, axis=0)
        u = jnp.where(valid, a * u_sh + u, u)
        a = jnp.where(valid, a * a_sh, a)
        sh *= 2
    hs = u + a * h_ref[...]
    h_ref[...] = hs[ts - 1:ts, :]

    y = hs * _gelu_tanh(yb)
    proj = _dot(_mx(y), wbr_ref[...])
    out_ref[...] = _sigmoid(gl) * proj


def _hgrn_constants(c, d):
    row = lax.broadcasted_iota(jnp.int32, (c, d), 0)
    sub = row & (SUBLANES - 1)
    pair_xor = (lax.broadcasted_iota(jnp.int32, (c, c), 0)
                ^ lax.broadcasted_iota(jnp.int32, (c, c), 1))
    cst = dict(scan=[sub >= sh for sh in (1, 2, 4)], lo4=sub < 4, odd=(row & 1) == 1,
               second={s: (row & (s - 1)) >= s // 2 for s in (4, 8)},
               same_block={}, sign={})
    s = c // 2
    while s >= 2:
        cst["same_block"][s] = pair_xor < s
        s //= 2
    for s in (4, 8):
        cst["sign"][s] = jnp.where(cst["second"][s], 1.0, -1.0).astype(F32)
    return cst


def _hgrn_chunk(q, k, v, gl, st_ref, cst, c_ref, b_ref, heads):
    c, d = q.shape
    dk = d // heads
    ngrp = c // SUBLANES
    hcols = [slice(hd * dk, (hd + 1) * dk) for hd in range(heads)]

    cs = gl
    for sh, m in zip((1, 2, 4), cst["scan"]):
        cs = cs + jnp.where(m, pltpu.roll(cs, sh, axis=0), 0.0)
    c_ref[...] = cs
    off = jnp.zeros((1, d), F32)
    parts = []
    for g in range(ngrp):
        parts.append((cs[g * SUBLANES:(g + 1) * SUBLANES, :] + off) * LOG2E)
        off = off + c_ref[g * SUBLANES + SUBLANES - 1:(g + 1) * SUBLANES, :]
    b2 = jnp.concatenate(parts, axis=0)
    b_ref[...] = b2
    b2_last = b_ref[c - 1:c, :]

    def grp(arr, g0, n):
        return arr[g0 * SUBLANES:(g0 + n) * SUBLANES, :]

    def ref_row(r):
        return b_ref[r:r + 1, :]

    attn = [None] * heads

    def add_level(qs, ks, s):
        qm, km = _mx(qs), _mx(ks)
        for hd in range(heads):
            a = _dot_nt(qm[:, hcols[hd]], km[:, hcols[hd]])
            attn[hd] = a if attn[hd] is None else jnp.where(cst["same_block"][s], a, attn[hd])

    s = c
    while s >= 2 * SUBLANES:
        half = s // 2
        hg = half // SUBLANES
        qs, ks = [], []
        for blk in range(c // s):
            g0 = blk * (s // SUBLANES)
            rr = ref_row(blk * s + half - 1)
            ks += [grp(k, g0, hg) * jnp.exp2(rr - grp(b2, g0, hg)), jnp.zeros((half, d), F32)]
            qs += [jnp.zeros((half, d), F32),
                   grp(q, g0 + hg, hg) * jnp.exp2(grp(b2, g0 + hg, hg) - rr)]
        add_level(jnp.concatenate(qs, axis=0), jnp.concatenate(ks, axis=0), s)
        s = half
    for s in (8, 4):
        if s == 8:
            rows = jnp.concatenate([jnp.broadcast_to(ref_row(g * SUBLANES + 3), (SUBLANES, d))
                                    for g in range(ngrp)], axis=0)
        else:
            lo = jnp.concatenate([jnp.broadcast_to(ref_row(g * SUBLANES + 1), (SUBLANES, d))
                                  for g in range(ngrp)], axis=0)
            hi = jnp.concatenate([jnp.broadcast_to(ref_row(g * SUBLANES + 5), (SUBLANES, d))
                                  for g in range(ngrp)], axis=0)
            rows = jnp.where(cst["lo4"], lo, hi)
        decay = jnp.exp2((b2 - rows) * cst["sign"][s])
        second = cst["second"][s]
        add_level(jnp.where(second, q * decay, 0.0), jnp.where(second, 0.0, k * decay), s)
    odd = cst["odd"]
    add_level(jnp.where(odd, q * jnp.exp2(gl * LOG2E), 0.0), jnp.where(odd, 0.0, k), 2)

    qk = q * k
    qe = _mx(q * jnp.exp2(b2))
    ke = _mx(k * jnp.exp2(b2_last - b2))
    vm = _mx(v)
    dec = jnp.exp2(b2_last)
    outs = []
    for hd in range(heads):
        hc = hcols[hd]
        st = st_ref[hd]
        diag = jnp.sum(qk[:, hc], axis=-1, keepdims=True)
        outs.append(_dot_nt(qe[:, hc], _mx(st)) + _dot(_mx(attn[hd]), vm[:, hc]) + diag * v[:, hc])
        st_ref[hd] = st * dec[:, hc] + _dot_tn(vm[:, hc], ke[:, hc])
    return outs


def _hgrn_kernel(layer, x_ref, gpre_ref, wq_ref, wf_ref, wi_ref, wgo_ref, wg_in_ref, lbraw_ref,
                 ng_ref, wbr_ref, acc_ref, out_ref, st_ref, y_ref, c_ref, b_ref):
    ts, d = x_ref.shape
    dk = d // HGRN_HEADS
    chunk = min(HGRN_CHUNK, ts)

    @pl.when(pl.program_id(1) == 0)
    def _():
        st_ref[...] = jnp.zeros_like(st_ref)

    n_layers = lbraw_ref.shape[0]
    rows = [lbraw_ref[r:r + 1, :] for r in range(n_layers)]
    mx = functools.reduce(jnp.maximum, rows)
    es = [jnp.exp(r - mx) for r in rows]
    tot = functools.reduce(lambda p, t: p + t, es)
    soft = [e / tot for e in es]
    lb = functools.reduce(lambda p, t: p + t, soft[:layer + 1]) - soft[0]

    x = x_ref[...]
    h = _mx(_rms(x, gpre_ref[...]))
    q = _dot(h, wq_ref[...])
    f = _dot(h, wf_ref[...])
    v = _dot(h, wi_ref[...])
    go = _dot(h, wgo_ref[...])
    gl = _dot(h, wg_in_ref[...])

    qf = q * _sigmoid(q) * (float(dk) ** -0.5)
    sg = _sigmoid(jnp.clip(f, -GATE_CLIP, GATE_CLIP))
    logf = jnp.minimum(jnp.log(lb + (1.0 - lb) * sg), 0.0)
    kf = jnp.maximum((1.0 - lb) * (1.0 - sg), 0.0)
    gg = go * _sigmoid(go)
    cst = _hgrn_constants(chunk, d)

    for ci in range(ts // chunk):
        r0 = ci * chunk
        outs = _hgrn_chunk(qf[r0:r0 + chunk, :], kf[r0:r0 + chunk, :], v[r0:r0 + chunk, :],
                           logf[r0:r0 + chunk, :], st_ref, cst, c_ref, b_ref, HGRN_HEADS)
        for hd, o in enumerate(outs):
            c0 = hd * dk
            on = o * lax.rsqrt(jnp.mean(o * o, axis=-1, keepdims=True) + EPS) * ng_ref[...]
            y_ref[r0:r0 + chunk, c0:c0 + dk] = on * gg[r0:r0 + chunk, c0:c0 + dk]

    proj = _dot(_mx(y_ref[...]), wbr_ref[...])
    out_ref[...] = acc_ref[...] + _sigmoid(gl) * proj


def _kv_kernel(mem_ref, g_ref, wk_ref, wv_ref, k_ref, v_ref):
    mn = _mx(_rms(mem_ref[...], g_ref[...]))
    k_ref[...] = _dot(mn, wk_ref[...]).astype(k_ref.dtype)
    v_ref[...] = _dot(mn, wv_ref[...]).astype(v_ref.dtype)


def _tail_kernel(n_seq, x_ref, gpre_ref, wa_ref, wgl_ref, wg_in_ref, dww_ref, dwb_ref, lng_ref,
                 lnb_ref, wbr_ref, acc_ref, wout_ref, gpost_ref,
                 gxa_ref, wq_ref, k_ref, v_ref, wo_ref, gxa_post_ref,
                 gmlp_ref, w1_ref, w2_ref, gmlp_post_ref,
                 out_ref, xmix_ref, ext_ref, sh_ref, y_ref, o_ref):
    ts, d = x_ref.shape
    g = pl.program_id(0)
    nc = d // CONF_COLS
    n_sh = ts + CONF_PAD - SUBLANES
    hd_dim = d // XA_HEADS
    fc = w1_ref.shape[1] // nc

    @pl.when(g == 0)
    def _():
        xmix_ref[...] = jnp.zeros_like(xmix_ref)

    @pl.when(lax.rem(g, n_seq) == 0)
    def _():
        ext_ref[0:CONF_PAD, :] = jnp.zeros((CONF_PAD, d), F32)

    xm = xmix_ref[...]
    hb = _mx(_rms(xm, gxa_ref[...]))
    q = _mx(_dot(hb, wq_ref[...]) * (float(hd_dim) ** -0.5))
    for hd in range(XA_HEADS):
        c0 = hd * hd_dim
        sc = _dot_nt(q[:, c0:c0 + hd_dim], k_ref[:, c0:c0 + hd_dim])
        pr = jnp.exp(sc - jnp.max(sc, axis=-1, keepdims=True))
        denom = jnp.sum(pr, axis=-1, keepdims=True)
        o_ref[:, c0:c0 + hd_dim] = _dot(_mx(pr), v_ref[:, c0:c0 + hd_dim]) / denom
    x2 = xm + _rms(_dot(_mx(o_ref[...]), wo_ref[...]), gxa_post_ref[...])
    h2 = _mx(_rms(x2, gmlp_ref[...]))

    x = x_ref[...]
    h = _mx(_rms(x, gpre_ref[...]))

    def project(c):
        cols = slice(c * CONF_COLS, (c + 1) * CONF_COLS)
        return _dot(h, wa_ref[:, cols]), _dot(h, wgl_ref[:, cols])

    cur = project(0)
    ffacc = None
    for c in range(nc):
        cols = slice(c * CONF_COLS, (c + 1) * CONF_COLS)
        nxt = project(c + 1) if c + 1 < nc else None
        a = jnp.maximum(_dot(h2, w1_ref[:, c * fc:(c + 1) * fc]), 0.0)
        part = _dot(_mx(a * a), w2_ref[c * fc:(c + 1) * fc, :])
        ffacc = part if ffacc is None else ffacc + part

        va, vg = cur
        u = va * _sigmoid(vg)
        ext_ref[CONF_PAD:CONF_PAD + ts, cols] = u
        for r in range(1, SUBLANES):
            sh_ref[r - 1] = ext_ref[pl.ds(SUBLANES - r, n_sh), cols]
        y = dwb_ref[:, cols] + dww_ref[CONF_WIDTH - 1:CONF_WIDTH, cols] * u
        for kk in range(CONF_WIDTH - 1):
            off = CONF_PAD - (CONF_WIDTH - 1) + kk
            r = (-off) % SUBLANES
            if r == 0:
                tap = ext_ref[pl.ds(off, ts), cols]
            else:
                tap = sh_ref[r - 1, pl.ds(off - (SUBLANES - r), ts), :]
            y = y + dww_ref[kk:kk + 1, cols] * tap
        ext_ref[0:CONF_PAD, cols] = ext_ref[ts:ts + CONF_PAD, cols]
        y_ref[:, cols] = y
        cur = nxt
    out_ref[...] = x2 + _rms(ffacc, gmlp_post_ref[...])

    gl = _dot(h, wg_in_ref[...])
    y = y_ref[...]
    mu = jnp.mean(y, axis=-1, keepdims=True)
    yc = y - mu
    var = jnp.mean(yc * yc, axis=-1, keepdims=True)
    yn = yc * lax.rsqrt(var + EPS) * lng_ref[...] + lnb_ref[...]
    proj = _dot(_mx(yn * _sigmoid(yn)), wbr_ref[...])
    mixed = acc_ref[...] + _sigmoid(gl) * proj
    xmix_ref[...] = x + _rms(_dot(_mx(mixed), wout_ref[...]), gpost_ref[...])


def _params(n_axes):
    return pltpu.CompilerParams(dimension_semantics=("arbitrary",) * n_axes,
                                vmem_limit_bytes=VMEM_LIMIT_BYTES)


def _resident(block_shape, index_map):
    return pl.BlockSpec(block_shape, index_map, pipeline_mode=pl.Buffered(1))


def _seq_tile(s, want):
    t = min(want, s)
    assert s % t == 0 and t % SUBLANES == 0
    return t


def _layer(x, mem, layer, p):
    b, s, d = x.shape
    m = mem.shape[1]
    ts = _seq_tile(s, MIX_TILE)
    chunk = min(HGRN_CHUNK, ts)
    assert ts % chunk == 0 and chunk >= 2 * SUBLANES and d % (HGRN_HEADS * LANES) == 0
    assert d % CONF_COLS == 0
    dk = d // HGRN_HEADS
    n_seq = s // ts
    grid = (b, n_seq)
    tok = pl.BlockSpec((None, ts, d), lambda bi, si: (bi, si, 0))
    tok_shape = jax.ShapeDtypeStruct((b, s, d), F32)

    def lay(arr):
        return _resident((None,) + arr.shape[1:], lambda *g: (layer,) + (0,) * (arr.ndim - 1))

    def win(col):
        return _resident((None, d, d), lambda *g: (layer, 0, col))

    def wbr(n):
        return _resident((None, None, d, d), lambda *g: (layer, n, 0, 0))

    wsq = _resident((None, d, d), lambda *g: (layer, 0, 0))

    acc = pl.pallas_call(
        _lru_kernel, grid=grid, out_shape=tok_shape,
        in_specs=[tok, lay(p["g_pre_mix"]), win(COL_LRU_X), win(COL_LRU_Y), win(COL_GATE0 + 0),
                  lay(p["lru_conv_w"]), lay(p["lru_conv_b"]), lay(p["lru_wx"]), lay(p["lru_bx"]),
                  lay(p["lru_wa"]), lay(p["lru_ba"]), lay(p["lru_lambda"]), wbr(0)],
        out_specs=tok,
        scratch_shapes=[pltpu.VMEM((ts + LRU_PAD, d), F32), pltpu.VMEM((1, d), F32)],
        compiler_params=_params(2), name="mix_lru",
    )(x, p["g_pre_mix"], p["w_in"], p["w_in"], p["w_in"], p["lru_conv_w"], p["lru_conv_b"],
      p["lru_wx"], p["lru_bx"], p["lru_wa"], p["lru_ba"], p["lru_lambda"], p["w_branch"])

    acc = pl.pallas_call(
        functools.partial(_hgrn_kernel, layer), grid=grid, out_shape=tok_shape,
        in_specs=[tok, lay(p["g_pre_mix"]), win(COL_HQ), win(COL_HF), win(COL_HI), win(COL_HG),
                  win(COL_GATE0 + 1),
                  _resident(p["hgrn_lb"].shape, lambda bi, si: (0, 0)),
                  lay(p["hgrn_norm_g"]), wbr(1), tok],
        out_specs=tok,
        scratch_shapes=[pltpu.VMEM((HGRN_HEADS, dk, dk), F32), pltpu.VMEM((ts, d), F32),
                        pltpu.VMEM((chunk, d), F32), pltpu.VMEM((chunk, d), F32)],
        compiler_params=_params(2), name="mix_hgrn",
    )(x, p["g_pre_mix"], p["w_in"], p["w_in"], p["w_in"], p["w_in"], p["w_in"], p["hgrn_lb"],
      p["hgrn_norm_g"], p["w_branch"], acc)

    mem_spec = pl.BlockSpec((None, m, d), lambda bi: (bi, 0, 0))
    k, v = pl.pallas_call(
        _kv_kernel, grid=(b,),
        out_shape=[jax.ShapeDtypeStruct((b, m, d), MXU_DTYPE)] * 2,
        in_specs=[mem_spec, lay(p["g_mem"]), wsq, _resident((None, d, d), lambda bi: (layer, 0, 1))],
        out_specs=[mem_spec, mem_spec],
        compiler_params=_params(1), name="xa_kv",
    )(mem, p["g_mem"], p["xa_wkv"], p["xa_wkv"])

    n_tiles = b * n_seq
    xf = x.reshape(b * s, d)
    accf = acc.reshape(b * s, d)
    cur_tile = pl.BlockSpec((ts, d), lambda g: (jnp.minimum(g, n_tiles - 1), 0))
    prev_tile = pl.BlockSpec((ts, d), lambda g: (jnp.maximum(g - 1, 0), 0))
    kv_spec = pl.BlockSpec((None, m, d), lambda g: (jnp.maximum(g - 1, 0) // n_seq, 0, 0))
    out = pl.pallas_call(
        functools.partial(_tail_kernel, n_seq), grid=(n_tiles + 1,),
        out_shape=jax.ShapeDtypeStruct((b * s, d), F32),
        in_specs=[cur_tile, lay(p["g_pre_mix"]), win(COL_CA), win(COL_CG), win(COL_GATE0 + 2),
                  lay(p["conf_dw_w"]), lay(p["conf_dw_b"]), lay(p["conf_ln_g"]), lay(p["conf_ln_b"]),
                  wbr(2), cur_tile, wsq, lay(p["g_post_mix"]),
                  lay(p["g_pre_xa"]), wsq, kv_spec, kv_spec, wsq, lay(p["g_post_xa"]),
                  lay(p["g_pre_mlp"]), lay(p["mlp_w1"]), lay(p["mlp_w2"]), lay(p["g_post_mlp"])],
        out_specs=prev_tile,
        scratch_shapes=[pltpu.VMEM((ts, d), F32), pltpu.VMEM((ts + CONF_PAD, d), F32),
                        pltpu.VMEM((SUBLANES - 1, ts + CONF_PAD - SUBLANES, CONF_COLS), F32),
                        pltpu.VMEM((ts, d), F32), pltpu.VMEM((ts, d), F32)],
        compiler_params=_params(1), name="tail",
    )(xf, p["g_pre_mix"], p["w_in"], p["w_in"], p["w_in"], p["conf_dw_w"], p["conf_dw_b"],
      p["conf_ln_g"], p["conf_ln_b"], p["w_branch"], accf, p["w_mix_out"], p["g_post_mix"],
      p["g_pre_xa"], p["xa_wq"], k, v, p["xa_wo"], p["g_post_xa"],
      p["g_pre_mlp"], p["mlp_w1"], p["mlp_w2"], p["g_post_mlp"])
    return out.reshape(b, s, d)


def kernel(x, mem, hgrn_lower_bounds, norm_pre_mix, norm_post_mix, w_in, lru_conv_w, lru_conv_b, lru_wx, lru_bx, lru_wa, lru_ba, lru_lambda, hgrn_norm_g, conf_dw_w, conf_dw_b, conf_ln_g, conf_ln_b, w_branch, w_mix_out, norm_pre_xa, norm_post_xa, mem_norm_g, xa_wq, xa_wkv, xa_wo, norm_pre_mlp, norm_post_mlp, mlp_w1, mlp_w2):
    depth = w_in.shape[0]
    row = lambda a: a[:, None, :].astype(F32)
    p = dict(
        g_pre_mix=row(norm_pre_mix), g_post_mix=row(norm_post_mix), w_in=_mx(w_in),
        lru_conv_w=lru_conv_w.astype(F32), lru_conv_b=row(lru_conv_b),
        lru_wx=_mx(lru_wx), lru_bx=row(lru_bx), lru_wa=_mx(lru_wa), lru_ba=row(lru_ba),
        lru_lambda=row(lru_lambda), hgrn_lb=hgrn_lower_bounds.astype(F32),
        hgrn_norm_g=row(hgrn_norm_g), conf_dw_w=conf_dw_w.astype(F32), conf_dw_b=row(conf_dw_b),
        conf_ln_g=row(conf_ln_g), conf_ln_b=row(conf_ln_b), w_branch=_mx(w_branch),
        w_mix_out=_mx(w_mix_out), g_pre_xa=row(norm_pre_xa), g_post_xa=row(norm_post_xa),
        g_mem=row(mem_norm_g), xa_wq=_mx(xa_wq), xa_wkv=_mx(xa_wkv), xa_wo=_mx(xa_wo),
        g_pre_mlp=row(norm_pre_mlp), g_post_mlp=row(norm_post_mlp),
        mlp_w1=_mx(mlp_w1), mlp_w2=_mx(mlp_w2))
    for layer in range(depth):
        x = _layer(x, mem, layer, p)
    return x
```

```python
import functools

import jax
import jax.numpy as jnp
from jax import lax
from jax.experimental import pallas as pl
from jax.experimental.pallas import tpu as pltpu

F32 = jnp.float32
MXU_DTYPE = jnp.bfloat16

EPS = 1e-6
LRU_HEADS = 4
LRU_CONV_WIDTH = 4
LRU_C = 8.0
HGRN_HEADS = 8
GATE_CLIP = 30.0
CONF_WIDTH = 31
XA_HEADS = 4
LOG2E = 1.4426950408889634
CONF_COLS = 256

LANES = 128
SUBLANES = 8
VMEM_LIMIT_BYTES = 56 * 1024 * 1024

BRANCH_TILE = 512
TAIL_TILE = 256
HGRN_CHUNK = 128
CONF_PAD = 32
LRU_PAD = 8

COL_LRU_X, COL_LRU_Y, COL_HQ, COL_HF, COL_HI, COL_HG, COL_CA, COL_CG, COL_GATE0 = range(9)


def _dot(a, b):
    return jnp.dot(a, b, preferred_element_type=F32)


def _dot_nt(a, b):
    return lax.dot_general(a, b, (((1,), (1,)), ((), ())), preferred_element_type=F32)


def _dot_tn(a, b):
    return lax.dot_general(a, b, (((0,), (0,)), ((), ())), preferred_element_type=F32)


def _mx(a):
    return a.astype(MXU_DTYPE)


def _rms(x, g):
    return x * lax.rsqrt(jnp.mean(x * x, axis=-1, keepdims=True) + EPS) * g


def _sigmoid(x):
    return jax.nn.sigmoid(x)


def _softplus(y):
    return jnp.maximum(y, 0.0) + jnp.log1p(jnp.exp(-jnp.abs(y)))


def _gelu_tanh(x):
    c = 0.7978845608028654
    return 0.5 * x * (1.0 + jnp.tanh(c * (x + 0.044715 * (x * x * x))))


def _lru_kernel(x_ref, gpre_ref, wx_in_ref, wy_in_ref, wg_in_ref, cw_ref, cb_ref,
                wgx_ref, bgx_ref, wga_ref, bga_ref, lam_ref, wbr_ref,
                out_ref, ext_ref, h_ref):
    ts, d = x_ref.shape
    blk = d // LRU_HEADS

    @pl.when(pl.program_id(1) == 0)
    def _():
        ext_ref[0:LRU_PAD, :] = jnp.zeros((LRU_PAD, d), F32)
        h_ref[...] = jnp.zeros_like(h_ref)

    x = x_ref[...]
    h = _mx(_rms(x, gpre_ref[...]))
    xb = _dot(h, wx_in_ref[...])
    yb = _dot(h, wy_in_ref[...])
    gl = _dot(h, wg_in_ref[...])

    ext_ref[LRU_PAD:LRU_PAD + ts, :] = xb
    xc = cb_ref[...] + cw_ref[LRU_CONV_WIDTH - 1:LRU_CONV_WIDTH, :] * xb
    for k in range(LRU_CONV_WIDTH - 1):
        off = LRU_PAD - (LRU_CONV_WIDTH - 1) + k
        xc = xc + cw_ref[k:k + 1, :] * ext_ref[pl.ds(off, ts), :]
    ext_ref[0:LRU_PAD, :] = ext_ref[ts:ts + LRU_PAD, :]

    xcm = _mx(xc)
    gx = jnp.concatenate(
        [_dot(xcm[:, i * blk:(i + 1) * blk], wgx_ref[i]) for i in range(LRU_HEADS)], axis=1)
    ga = jnp.concatenate(
        [_dot(xcm[:, i * blk:(i + 1) * blk], wga_ref[i]) for i in range(LRU_HEADS)], axis=1)
    gate_x = _sigmoid(gx + bgx_ref[...])
    gate_a = _sigmoid(ga + bga_ref[...])

    log_a = jnp.minimum(-LRU_C * gate_a * _softplus(-lam_ref[...]), -1e-6)
    a = jnp.exp(log_a)
    m = -jnp.tanh(log_a) * (a * a + 1.0)
    u = xc * gate_x * (m * lax.rsqrt(m))

    row = lax.broadcasted_iota(jnp.int32, (ts, d), 0)
    sh = 1
    while sh < ts:
        valid = row >= sh
        a_sh = pltpu.roll(a, sh, axis=0)
        u_sh = pltpu.roll(u, sh, axis=0)
        u = jnp.where(valid, a * u_sh + u, u)
        a = jnp.where(valid, a * a_sh, a)
        sh *= 2
    hs = u + a * h_ref[...]
    h_ref[...] = hs[ts - 1:ts, :]

    y = hs * _gelu_tanh(yb)
    proj = _dot(_mx(y), wbr_ref[...])
    out_ref[...] = _sigmoid(gl) * proj


def _hgrn_constants(c, d):
    row = lax.broadcasted_iota(jnp.int32, (c, d), 0)
    sub = row & (SUBLANES - 1)
    pair_xor = (lax.broadcasted_iota(jnp.int32, (c, c), 0)
                ^ lax.broadcasted_iota(jnp.int32, (c, c), 1))
    cst = dict(scan=[sub >= sh for sh in (1, 2, 4)], lo4=sub < 4, odd=(row & 1) == 1,
               second={s: (row & (s - 1)) >= s // 2 for s in (4, 8)},
               same_block={}, sign={})
    s = c // 2
    while s >= 2:
        cst["same_block"][s] = pair_xor < s
        s //= 2
    for s in (4, 8):
        cst["sign"][s] = jnp.where(cst["second"][s], 1.0, -1.0).astype(F32)
    return cst


def _hgrn_chunk(q, k, v, gl, st_ref, cst, c_ref, b_ref, heads):
    c, d = q.shape
    dk = d // heads
    ngrp = c // SUBLANES
    hcols = [slice(hd * dk, (hd + 1) * dk) for hd in range(heads)]

    cs = gl
    for sh, m in zip((1, 2, 4), cst["scan"]):
        cs = cs + jnp.where(m, pltpu.roll(cs, sh, axis=0), 0.0)
    c_ref[...] = cs
    off = jnp.zeros((1, d), F32)
    parts = []
    for g in range(ngrp):
        parts.append((cs[g * SUBLANES:(g + 1) * SUBLANES, :] + off) * LOG2E)
        off = off + c_ref[g * SUBLANES + SUBLANES - 1:(g + 1) * SUBLANES, :]
    b2 = jnp.concatenate(parts, axis=0)
    b_ref[...] = b2
    b2_last = b_ref[c - 1:c, :]

    def grp(arr, g0, n):
        return arr[g0 * SUBLANES:(g0 + n) * SUBLANES, :]

    def ref_row(r):
        return b_ref[r:r + 1, :]

    attn = [None] * heads

    def add_level(qs, ks, s):
        qm, km = _mx(qs), _mx(ks)
        for hd in range(heads):
            a = _dot_nt(qm[:, hcols[hd]], km[:, hcols[hd]])
            attn[hd] = a if attn[hd] is None else jnp.where(cst["same_block"][s], a, attn[hd])

    s = c
    while s >= 2 * SUBLANES:
        half = s // 2
        hg = half // SUBLANES
        qs, ks = [], []
        for blk in range(c // s):
            g0 = blk * (s // SUBLANES)
            rr = ref_row(blk * s + half - 1)
            ks += [grp(k, g0, hg) * jnp.exp2(rr - grp(b2, g0, hg)), jnp.zeros((half, d), F32)]
            qs += [jnp.zeros((half, d), F32),
                   grp(q, g0 + hg, hg) * jnp.exp2(grp(b2, g0 + hg, hg) - rr)]
        add_level(jnp.concatenate(qs, axis=0), jnp.concatenate(ks, axis=0), s)
        s = half
    for s in (8, 4):
        if s == 8:
            rows = jnp.concatenate([jnp.broadcast_to(ref_row(g * SUBLANES + 3), (SUBLANES, d))
                                    for g in range(ngrp)], axis=0)
        else:
            lo = jnp.concatenate([jnp.broadcast_to(ref_row(g * SUBLANES + 1), (SUBLANES, d))
                                  for g in range(ngrp)], axis=0)
            hi = jnp.concatenate([jnp.broadcast_to(ref_row(g * SUBLANES + 5), (SUBLANES, d))
                                  for g in range(ngrp)], axis=0)
            rows = jnp.where(cst["lo4"], lo, hi)
        decay = jnp.exp2((b2 - rows) * cst["sign"][s])
        second = cst["second"][s]
        add_level(jnp.where(second, q * decay, 0.0), jnp.where(second, 0.0, k * decay), s)
    odd = cst["odd"]
    add_level(jnp.where(odd, q * jnp.exp2(gl * LOG2E), 0.0), jnp.where(odd, 0.0, k), 2)

    qk = q * k
    qe = _mx(q * jnp.exp2(b2))
    ke = _mx(k * jnp.exp2(b2_last - b2))
    vm = _mx(v)
    dec = jnp.exp2(b2_last)
    outs = []
    for hd in range(heads):
        hc = hcols[hd]
        st = st_ref[hd]
        diag = jnp.sum(qk[:, hc], axis=-1, keepdims=True)
        outs.append(_dot_nt(qe[:, hc], _mx(st)) + _dot(_mx(attn[hd]), vm[:, hc]) + diag * v[:, hc])
        st_ref[hd] = st * dec[:, hc] + _dot_tn(vm[:, hc], ke[:, hc])
    return outs


def _hgrn_kernel(layer, x_ref, gpre_ref, wq_ref, wf_ref, wi_ref, wgo_ref, wg_in_ref, lbraw_ref,
                 ng_ref, wbr_ref, acc_ref, out_ref, st_ref, y_ref, c_ref, b_ref):
    ts, d = x_ref.shape
    dk = d // HGRN_HEADS
    chunk = min(HGRN_CHUNK, ts)

    @pl.when(pl.program_id(1) == 0)
    def _():
        st_ref[...] = jnp.zeros_like(st_ref)

    n_layers = lbraw_ref.shape[0]
    rows = [lbraw_ref[r:r + 1, :] for r in range(n_layers)]
    mx = functools.reduce(jnp.maximum, rows)
    es = [jnp.exp(r - mx) for r in rows]
    tot = functools.reduce(lambda p, t: p + t, es)
    soft = [e / tot for e in es]
    lb = functools.reduce(lambda p, t: p + t, soft[:layer + 1]) - soft[0]

    x = x_ref[...]
    h = _mx(_rms(x, gpre_ref[...]))
    q = _dot(h, wq_ref[...])
    f = _dot(h, wf_ref[...])
    v = _dot(h, wi_ref[...])
    go = _dot(h, wgo_ref[...])
    gl = _dot(h, wg_in_ref[...])

    qf = q * _sigmoid(q) * (float(dk) ** -0.5)
    sg = _sigmoid(jnp.clip(f, -GATE_CLIP, GATE_CLIP))
    logf = jnp.minimum(jnp.log(lb + (1.0 - lb) * sg), 0.0)
    kf = jnp.maximum((1.0 - lb) * (1.0 - sg), 0.0)
    gg = go * _sigmoid(go)
    cst = _hgrn_constants(chunk, d)

    for ci in range(ts // chunk):
        r0 = ci * chunk
        outs = _hgrn_chunk(qf[r0:r0 + chunk, :], kf[r0:r0 + chunk, :], v[r0:r0 + chunk, :],
                           logf[r0:r0 + chunk, :], st_ref, cst, c_ref, b_ref, HGRN_HEADS)
        for hd, o in enumerate(outs):
            c0 = hd * dk
            on = o * lax.rsqrt(jnp.mean(o * o, axis=-1, keepdims=True) + EPS) * ng_ref[...]
            y_ref[r0:r0 + chunk, c0:c0 + dk] = on * gg[r0:r0 + chunk, c0:c0 + dk]

    proj = _dot(_mx(y_ref[...]), wbr_ref[...])
    out_ref[...] = acc_ref[...] + _sigmoid(gl) * proj


def _kv_kernel(mem_ref, g_ref, wk_ref, wv_ref, k_ref, v_ref):
    mn = _mx(_rms(mem_ref[...], g_ref[...]))
    k_ref[...] = _dot(mn, wk_ref[...]).astype(k_ref.dtype)
    v_ref[...] = _dot(mn, wv_ref[...]).astype(v_ref.dtype)


def _tail_kernel(n_seq, x_ref, gpre_ref, wa_ref, wgl_ref, wg_in_ref, dww_ref, dwb_ref, lng_ref,
                 lnb_ref, wbr_ref, acc_ref, wout_ref, gpost_ref,
                 gxa_ref, wq_ref, k_ref, v_ref, wo_ref, gxa_post_ref,
                 gmlp_ref, w1_ref, w2_ref, gmlp_post_ref,
                 out_ref, xmix_ref, ext_ref, sh_ref, y_ref, o_ref):
    ts, d = x_ref.shape
    g = pl.program_id(0)
    nc = d // CONF_COLS
    n_sh = ts + CONF_PAD - SUBLANES
    hd_dim = d // XA_HEADS
    fc = w1_ref.shape[1] // nc

    @pl.when(g == 0)
    def _():
        xmix_ref[...] = jnp.zeros_like(xmix_ref)

    @pl.when(lax.rem(g, n_seq) == 0)
    def _():
        ext_ref[0:CONF_PAD, :] = jnp.zeros((CONF_PAD, d), F32)

    xm = xmix_ref[...]
    hb = _mx(_rms(xm, gxa_ref[...]))
    q = _mx(_dot(hb, wq_ref[...]) * (float(hd_dim) ** -0.5))
    for hd in range(XA_HEADS):
        c0 = hd * hd_dim
        sc = _dot_nt(q[:, c0:c0 + hd_dim], k_ref[:, c0:c0 + hd_dim])
        pr = jnp.exp(sc - jnp.max(sc, axis=-1, keepdims=True))
        denom = jnp.sum(pr, axis=-1, keepdims=True)
        o_ref[:, c0:c0 + hd_dim] = _dot(_mx(pr), v_ref[:, c0:c0 + hd_dim]) / denom
    x2 = xm + _rms(_dot(_mx(o_ref[...]), wo_ref[...]), gxa_post_ref[...])
    h2 = _mx(_rms(x2, gmlp_ref[...]))

    x = x_ref[...]
    h = _mx(_rms(x, gpre_ref[...]))

    def project(c):
        cols = slice(c * CONF_COLS, (c + 1) * CONF_COLS)
        return _dot(h, wa_ref[:, cols]), _dot(h, wgl_ref[:, cols])

    cur = project(0)
    ffacc = None
    for c in range(nc):
        cols = slice(c * CONF_COLS, (c + 1) * CONF_COLS)
        nxt = project(c + 1) if c + 1 < nc else None
        a = jnp.maximum(_dot(h2, w1_ref[:, c * fc:(c + 1) * fc]), 0.0)
        part = _dot(_mx(a * a), w2_ref[c * fc:(c + 1) * fc, :])
        ffacc = part if ffacc is None else ffacc + part

        va, vg = cur
        u = va * _sigmoid(vg)
        ext_ref[CONF_PAD:CONF_PAD + ts, cols] = u
        for r in range(1, SUBLANES):
            sh_ref[r - 1] = ext_ref[pl.ds(SUBLANES - r, n_sh), cols]
        y = dwb_ref[:, cols] + dww_ref[CONF_WIDTH - 1:CONF_WIDTH, cols] * u
        for kk in range(CONF_WIDTH - 1):
            off = CONF_PAD - (CONF_WIDTH - 1) + kk
            r = (-off) % SUBLANES
            if r == 0:
                tap = ext_ref[pl.ds(off, ts), cols]
            else:
                tap = sh_ref[r - 1, pl.ds(off - (SUBLANES - r), ts), :]
            y = y + dww_ref[kk:kk + 1, cols] * tap
        ext_ref[0:CONF_PAD, cols] = ext_ref[ts:ts + CONF_PAD, cols]
        y_ref[:, cols] = y
        cur = nxt
    out_ref[...] = x2 + _rms(ffacc, gmlp_post_ref[...])

    gl = _dot(h, wg_in_ref[...])
    y = y_ref[...]
    mu = jnp.mean(y, axis=-1, keepdims=True)
    yc = y - mu
    var = jnp.mean(yc * yc, axis=-1, keepdims=True)
    yn = yc * lax.rsqrt(var + EPS) * lng_ref[...] + lnb_ref[...]
    proj = _dot(_mx(yn * _sigmoid(yn)), wbr_ref[...])
    mixed = acc_ref[...] + _sigmoid(gl) * proj
    xmix_ref[...] = x + _rms(_dot(_mx(mixed), wout_ref[...]), gpost_ref[...])


def _params(n_axes):
    return pltpu.CompilerParams(dimension_semantics=("arbitrary",) * n_axes,
                                vmem_limit_bytes=VMEM_LIMIT_BYTES)


def _resident(block_shape, index_map):
    return pl.BlockSpec(block_shape, index_map, pipeline_mode=pl.Buffered(1))


def _seq_tile(s, want):
    t = min(want, s)
    assert s % t == 0 and t % SUBLANES == 0
    return t


def _layer(x, mem, layer, p):
    b, s, d = x.shape
    m = mem.shape[1]
    ts = _seq_tile(s, BRANCH_TILE)
    chunk = min(HGRN_CHUNK, ts)
    assert ts % chunk == 0 and chunk >= 2 * SUBLANES and d % (HGRN_HEADS * LANES) == 0
    assert d % CONF_COLS == 0
    dk = d // HGRN_HEADS
    grid = (b, s // ts)
    tok = pl.BlockSpec((None, ts, d), lambda bi, si: (bi, si, 0))
    tok_shape = jax.ShapeDtypeStruct((b, s, d), F32)

    def lay(arr):
        return _resident((None,) + arr.shape[1:], lambda *g: (layer,) + (0,) * (arr.ndim - 1))

    def win(col):
        return _resident((None, d, d), lambda *g: (layer, 0, col))

    def wbr(n):
        return _resident((None, None, d, d), lambda *g: (layer, n, 0, 0))

    wsq = _resident((None, d, d), lambda *g: (layer, 0, 0))

    acc = pl.pallas_call(
        _lru_kernel, grid=grid, out_shape=tok_shape,
        in_specs=[tok, lay(p["g_pre_mix"]), win(COL_LRU_X), win(COL_LRU_Y), win(COL_GATE0 + 0),
                  lay(p["lru_conv_w"]), lay(p["lru_conv_b"]), lay(p["lru_wx"]), lay(p["lru_bx"]),
                  lay(p["lru_wa"]), lay(p["lru_ba"]), lay(p["lru_lambda"]), wbr(0)],
        out_specs=tok,
        scratch_shapes=[pltpu.VMEM((ts + LRU_PAD, d), F32), pltpu.VMEM((1, d), F32)],
        compiler_params=_params(2), name="mix_lru",
    )(x, p["g_pre_mix"], p["w_in"], p["w_in"], p["w_in"], p["lru_conv_w"], p["lru_conv_b"],
      p["lru_wx"], p["lru_bx"], p["lru_wa"], p["lru_ba"], p["lru_lambda"], p["w_branch"])

    acc = pl.pallas_call(
        functools.partial(_hgrn_kernel, layer), grid=grid, out_shape=tok_shape,
        in_specs=[tok, lay(p["g_pre_mix"]), win(COL_HQ), win(COL_HF), win(COL_HI), win(COL_HG),
                  win(COL_GATE0 + 1),
                  _resident(p["hgrn_lb"].shape, lambda bi, si: (0, 0)),
                  lay(p["hgrn_norm_g"]), wbr(1), tok],
        out_specs=tok,
        scratch_shapes=[pltpu.VMEM((HGRN_HEADS, dk, dk), F32), pltpu.VMEM((ts, d), F32),
                        pltpu.VMEM((chunk, d), F32), pltpu.VMEM((chunk, d), F32)],
        compiler_params=_params(2), name="mix_hgrn",
    )(x, p["g_pre_mix"], p["w_in"], p["w_in"], p["w_in"], p["w_in"], p["w_in"], p["hgrn_lb"],
      p["hgrn_norm_g"], p["w_branch"], acc)

    mem_spec = pl.BlockSpec((None, m, d), lambda bi: (bi, 0, 0))
    k, v = pl.pallas_call(
        _kv_kernel, grid=(b,),
        out_shape=[jax.ShapeDtypeStruct((b, m, d), MXU_DTYPE)] * 2,
        in_specs=[mem_spec, lay(p["g_mem"]), wsq, _resident((None, d, d), lambda bi: (layer, 0, 1))],
        out_specs=[mem_spec, mem_spec],
        compiler_params=_params(1), name="xa_kv",
    )(mem, p["g_mem"], p["xa_wkv"], p["xa_wkv"])

    ts = _seq_tile(s, TAIL_TILE)
    n_seq = s // ts
    n_tiles = b * n_seq
    xf = x.reshape(b * s, d)
    accf = acc.reshape(b * s, d)
    cur_tile = pl.BlockSpec((ts, d), lambda g: (jnp.minimum(g, n_tiles - 1), 0))
    prev_tile = pl.BlockSpec((ts, d), lambda g: (jnp.maximum(g - 1, 0), 0))
    kv_spec = pl.BlockSpec((None, m, d), lambda g: (jnp.maximum(g - 1, 0) // n_seq, 0, 0))
    out = pl.pallas_call(
        functools.partial(_tail_kernel, n_seq), grid=(n_tiles + 1,),
        out_shape=jax.ShapeDtypeStruct((b * s, d), F32),
        in_specs=[cur_tile, lay(p["g_pre_mix"]), win(COL_CA), win(COL_CG), win(COL_GATE0 + 2),
                  lay(p["conf_dw_w"]), lay(p["conf_dw_b"]), lay(p["conf_ln_g"]), lay(p["conf_ln_b"]),
                  wbr(2), cur_tile, wsq, lay(p["g_post_mix"]),
                  lay(p["g_pre_xa"]), wsq, kv_spec, kv_spec, wsq, lay(p["g_post_xa"]),
                  lay(p["g_pre_mlp"]), lay(p["mlp_w1"]), lay(p["mlp_w2"]), lay(p["g_post_mlp"])],
        out_specs=prev_tile,
        scratch_shapes=[pltpu.VMEM((ts, d), F32), pltpu.VMEM((ts + CONF_PAD, d), F32),
                        pltpu.VMEM((SUBLANES - 1, ts + CONF_PAD - SUBLANES, CONF_COLS), F32),
                        pltpu.VMEM((ts, d), F32), pltpu.VMEM((ts, d), F32)],
        compiler_params=_params(1), name="tail",
    )(xf, p["g_pre_mix"], p["w_in"], p["w_in"], p["w_in"], p["conf_dw_w"], p["conf_dw_b"],
      p["conf_ln_g"], p["conf_ln_b"], p["w_branch"], accf, p["w_mix_out"], p["g_post_mix"],
      p["g_pre_xa"], p["xa_wq"], k, v, p["xa_wo"], p["g_post_xa"],
      p["g_pre_mlp"], p["mlp_w1"], p["mlp_w2"], p["g_post_mlp"])
    return out.reshape(b, s, d)


def kernel(x, mem, hgrn_lower_bounds, norm_pre_mix, norm_post_mix, w_in, lru_conv_w, lru_conv_b, lru_wx, lru_bx, lru_wa, lru_ba, lru_lambda, hgrn_norm_g, conf_dw_w, conf_dw_b, conf_ln_g, conf_ln_b, w_branch, w_mix_out, norm_pre_xa, norm_post_xa, mem_norm_g, xa_wq, xa_wkv, xa_wo, norm_pre_mlp, norm_post_mlp, mlp_w1, mlp_w2):
    depth = w_in.shape[0]
    row = lambda a: a[:, None, :].astype(F32)
    p = dict(
        g_pre_mix=row(norm_pre_mix), g_post_mix=row(norm_post_mix), w_in=_mx(w_in),
        lru_conv_w=lru_conv_w.astype(F32), lru_conv_b=row(lru_conv_b),
        lru_wx=_mx(lru_wx), lru_bx=row(lru_bx), lru_wa=_mx(lru_wa), lru_ba=row(lru_ba),
        lru_lambda=row(lru_lambda), hgrn_lb=hgrn_lower_bounds.astype(F32),
        hgrn_norm_g=row(hgrn_norm_g), conf_dw_w=conf_dw_w.astype(F32), conf_dw_b=row(conf_dw_b),
        conf_ln_g=row(conf_ln_g), conf_ln_b=row(conf_ln_b), w_branch=_mx(w_branch),
        w_mix_out=_mx(w_mix_out), g_pre_xa=row(norm_pre_xa), g_post_xa=row(norm_post_xa),
        g_mem=row(mem_norm_g), xa_wq=_mx(xa_wq), xa_wkv=_mx(xa_wkv), xa_wo=_mx(xa_wo),
        g_pre_mlp=row(norm_pre_mlp), g_post_mlp=row(norm_post_mlp),
        mlp_w1=_mx(mlp_w1), mlp_w2=_mx(mlp_w2))
    for layer in range(depth):
        x = _layer(x, mem, layer, p)
    return x
```

```python
import functools

import jax
import jax.numpy as jnp
from jax import lax
from jax.experimental import pallas as pl
from jax.experimental.pallas import tpu as pltpu

F32 = jnp.float32
MXU_DTYPE = jnp.bfloat16

EPS = 1e-6
LRU_HEADS = 4
LRU_CONV_WIDTH = 4
LRU_C = 8.0
HGRN_HEADS = 8
GATE_CLIP = 30.0
CONF_WIDTH = 31
XA_HEADS = 4
LOG2E = 1.4426950408889634
CONF_COLS = 256

LANES = 128
SUBLANES = 8
VMEM_LIMIT_BYTES = 56 * 1024 * 1024

BRANCH_TILE = 512
TAIL_TILE = 256
HGRN_CHUNK = 128
CONF_PAD = 32
LRU_PAD = 8

COL_LRU_X, COL_LRU_Y, COL_HQ, COL_HF, COL_HI, COL_HG, COL_CA, COL_CG, COL_GATE0 = range(9)


def _dot(a, b):
    return jnp.dot(a, b, preferred_element_type=F32)


def _dot_nt(a, b):
    return lax.dot_general(a, b, (((1,), (1,)), ((), ())), preferred_element_type=F32)


def _dot_tn(a, b):
    return lax.dot_general(a, b, (((0,), (0,)), ((), ())), preferred_element_type=F32)


def _mx(a):
    return a.astype(MXU_DTYPE)


def _rms(x, g):
    return x * lax.rsqrt(jnp.mean(x * x, axis=-1, keepdims=True) + EPS) * g


def _sigmoid(x):
    return jax.nn.sigmoid(x)


def _softplus(y):
    return jnp.maximum(y, 0.0) + jnp.log1p(jnp.exp(-jnp.abs(y)))


def _gelu_tanh(x):
    c = 0.7978845608028654
    return 0.5 * x * (1.0 + jnp.tanh(c * (x + 0.044715 * (x * x * x))))


def _lru_kernel(n_seq, x_ref, gpre_ref, wx_in_ref, wy_in_ref, wg_in_ref, cw_ref, cb_ref,
                wgx_ref, bgx_ref, wga_ref, bga_ref, lam_ref, wbr_ref,
                out_ref, z_ref, ext_ref, h_ref):
    ts, d = x_ref.shape
    blk = d // LRU_HEADS
    g = pl.program_id(0)

    @pl.when(g == 0)
    def _():
        z_ref[...] = jnp.zeros_like(z_ref)

    @pl.when(jnp.logical_or(g == 0, lax.rem(g + (n_seq - 1), n_seq) == 0))
    def _():
        ext_ref[0:LRU_PAD, :] = jnp.zeros((LRU_PAD, d), F32)
        h_ref[...] = jnp.zeros_like(h_ref)

    xb = z_ref[0]
    yb = z_ref[1]
    gl = z_ref[2]

    ext_ref[LRU_PAD:LRU_PAD + ts, :] = xb
    xc = cb_ref[...] + cw_ref[LRU_CONV_WIDTH - 1:LRU_CONV_WIDTH, :] * xb
    for k in range(LRU_CONV_WIDTH - 1):
        off = LRU_PAD - (LRU_CONV_WIDTH - 1) + k
        xc = xc + cw_ref[k:k + 1, :] * ext_ref[pl.ds(off, ts), :]
    ext_ref[0:LRU_PAD, :] = ext_ref[ts:ts + LRU_PAD, :]

    xcm = _mx(xc)
    gx = jnp.concatenate(
        [_dot(xcm[:, i * blk:(i + 1) * blk], wgx_ref[i]) for i in range(LRU_HEADS)], axis=1)
    ga = jnp.concatenate(
        [_dot(xcm[:, i * blk:(i + 1) * blk], wga_ref[i]) for i in range(LRU_HEADS)], axis=1)

    hn = _mx(_rms(x_ref[...], gpre_ref[...]))
    zx = _dot(hn, wx_in_ref[...])
    zy = _dot(hn, wy_in_ref[...])
    zg = _dot(hn, wg_in_ref[...])

    gate_x = _sigmoid(gx + bgx_ref[...])
    gate_a = _sigmoid(ga + bga_ref[...])

    log_a = jnp.minimum(-LRU_C * gate_a * _softplus(-lam_ref[...]), -1e-6)
    a = jnp.exp(log_a)
    m = -jnp.tanh(log_a) * (a * a + 1.0)
    u = xc * gate_x * (m * lax.rsqrt(m))

    row = lax.broadcasted_iota(jnp.int32, (ts, d), 0)
    sh = 1
    while sh < ts:
        valid = row >= sh
        a_sh = pltpu.roll(a, sh, axis=0)
        u_sh = pltpu.roll(u, sh, axis=0)
        u = jnp.where(valid, a * u_sh + u, u)
        a = jnp.where(valid, a * a_sh, a)
        sh *= 2
    hs = u + a * h_ref[...]
    h_ref[...] = hs[ts - 1:ts, :]

    y = hs * _gelu_tanh(yb)
    proj = _dot(_mx(y), wbr_ref[...])
    out_ref[...] = _sigmoid(gl) * proj

    z_ref[0] = zx
    z_ref[1] = zy
    z_ref[2] = zg


def _hgrn_constants(c, d):
    row = lax.broadcasted_iota(jnp.int32, (c, d), 0)
    sub = row & (SUBLANES - 1)
    pair_xor = (lax.broadcasted_iota(jnp.int32, (c, c), 0)
                ^ lax.broadcasted_iota(jnp.int32, (c, c), 1))
    cst = dict(scan=[sub >= sh for sh in (1, 2, 4)], lo4=sub < 4, odd=(row & 1) == 1,
               second={s: (row & (s - 1)) >= s // 2 for s in (4, 8)},
               same_block={}, sign={})
    s = c // 2
    while s >= 2:
        cst["same_block"][s] = pair_xor < s
        s //= 2
    for s in (4, 8):
        cst["sign"][s] = jnp.where(cst["second"][s], 1.0, -1.0).astype(F32)
    return cst


def _hgrn_chunk(q, k, v, gl, st_ref, cst, c_ref, b_ref, heads):
    c, d = q.shape
    dk = d // heads
    ngrp = c // SUBLANES
    hcols = [slice(hd * dk, (hd + 1) * dk) for hd in range(heads)]

    cs = gl
    for sh, m in zip((1, 2, 4), cst["scan"]):
        cs = cs + jnp.where(m, pltpu.roll(cs, sh, axis=0), 0.0)
    c_ref[...] = cs
    off = jnp.zeros((1, d), F32)
    parts = []
    for g in range(ngrp):
        parts.append((cs[g * SUBLANES:(g + 1) * SUBLANES, :] + off) * LOG2E)
        off = off + c_ref[g * SUBLANES + SUBLANES - 1:(g + 1) * SUBLANES, :]
    b2 = jnp.concatenate(parts, axis=0)
    b_ref[...] = b2
    b2_last = b_ref[c - 1:c, :]

    def grp(arr, g0, n):
        return arr[g0 * SUBLANES:(g0 + n) * SUBLANES, :]

    def ref_row(r):
        return b_ref[r:r + 1, :]

    attn = [None] * heads

    def add_level(qs, ks, s):
        qm, km = _mx(qs), _mx(ks)
        for hd in range(heads):
            a = _dot_nt(qm[:, hcols[hd]], km[:, hcols[hd]])
            attn[hd] = a if attn[hd] is None else jnp.where(cst["same_block"][s], a, attn[hd])

    s = c
    while s >= 2 * SUBLANES:
        half = s // 2
        hg = half // SUBLANES
        qs, ks = [], []
        for blk in range(c // s):
            g0 = blk * (s // SUBLANES)
            rr = ref_row(blk * s + half - 1)
            ks += [grp(k, g0, hg) * jnp.exp2(rr - grp(b2, g0, hg)), jnp.zeros((half, d), F32)]
            qs += [jnp.zeros((half, d), F32),
                   grp(q, g0 + hg, hg) * jnp.exp2(grp(b2, g0 + hg, hg) - rr)]
        add_level(jnp.concatenate(qs, axis=0), jnp.concatenate(ks, axis=0), s)
        s = half
    for s in (8, 4):
        if s == 8:
            rows = jnp.concatenate([jnp.broadcast_to(ref_row(g * SUBLANES + 3), (SUBLANES, d))
                                    for g in range(ngrp)], axis=0)
        else:
            lo = jnp.concatenate([jnp.broadcast_to(ref_row(g * SUBLANES + 1), (SUBLANES, d))
                                  for g in range(ngrp)], axis=0)
            hi = jnp.concatenate([jnp.broadcast_to(ref_row(g * SUBLANES + 5), (SUBLANES, d))
                                  for g in range(ngrp)], axis=0)
            rows = jnp.where(cst["lo4"], lo, hi)
        decay = jnp.exp2((b2 - rows) * cst["sign"][s])
        second = cst["second"][s]
        add_level(jnp.where(second, q * decay, 0.0), jnp.where(second, 0.0, k * decay), s)
    odd = cst["odd"]
    add_level(jnp.where(odd, q * jnp.exp2(gl * LOG2E), 0.0), jnp.where(odd, 0.0, k), 2)

    qk = q * k
    qe = _mx(q * jnp.exp2(b2))
    ke = _mx(k * jnp.exp2(b2_last - b2))
    vm = _mx(v)
    dec = jnp.exp2(b2_last)
    outs = []
    for hd in range(heads):
        hc = hcols[hd]
        st = st_ref[hd]
        diag = jnp.sum(qk[:, hc], axis=-1, keepdims=True)
        outs.append(_dot_nt(qe[:, hc], _mx(st)) + _dot(_mx(attn[hd]), vm[:, hc]) + diag * v[:, hc])
        st_ref[hd] = st * dec[:, hc] + _dot_tn(vm[:, hc], ke[:, hc])
    return outs


def _hgrn_kernel(layer, x_ref, gpre_ref, wq_ref, wf_ref, wi_ref, wgo_ref, wg_in_ref, lbraw_ref,
                 ng_ref, wbr_ref, acc_ref, out_ref, st_ref, y_ref, c_ref, b_ref):
    ts, d = x_ref.shape
    dk = d // HGRN_HEADS
    chunk = min(HGRN_CHUNK, ts)

    @pl.when(pl.program_id(1) == 0)
    def _():
        st_ref[...] = jnp.zeros_like(st_ref)

    n_layers = lbraw_ref.shape[0]
    rows = [lbraw_ref[r:r + 1, :] for r in range(n_layers)]
    mx = functools.reduce(jnp.maximum, rows)
    es = [jnp.exp(r - mx) for r in rows]
    tot = functools.reduce(lambda p, t: p + t, es)
    soft = [e / tot for e in es]
    lb = functools.reduce(lambda p, t: p + t, soft[:layer + 1]) - soft[0]

    x = x_ref[...]
    h = _mx(_rms(x, gpre_ref[...]))
    q = _dot(h, wq_ref[...])
    f = _dot(h, wf_ref[...])
    v = _dot(h, wi_ref[...])
    go = _dot(h, wgo_ref[...])
    gl = _dot(h, wg_in_ref[...])

    qf = q * _sigmoid(q) * (float(dk) ** -0.5)
    sg = _sigmoid(jnp.clip(f, -GATE_CLIP, GATE_CLIP))
    logf = jnp.minimum(jnp.log(lb + (1.0 - lb) * sg), 0.0)
    kf = jnp.maximum((1.0 - lb) * (1.0 - sg), 0.0)
    gg = go * _sigmoid(go)
    cst = _hgrn_constants(chunk, d)

    for ci in range(ts // chunk):
        r0 = ci * chunk
        outs = _hgrn_chunk(qf[r0:r0 + chunk, :], kf[r0:r0 + chunk, :], v[r0:r0 + chunk, :],
                           logf[r0:r0 + chunk, :], st_ref, cst, c_ref, b_ref, HGRN_HEADS)
        for hd, o in enumerate(outs):
            c0 = hd * dk
            on = o * lax.rsqrt(jnp.mean(o * o, axis=-1, keepdims=True) + EPS) * ng_ref[...]
            y_ref[r0:r0 + chunk, c0:c0 + dk] = on * gg[r0:r0 + chunk, c0:c0 + dk]

    proj = _dot(_mx(y_ref[...]), wbr_ref[...])
    out_ref[...] = acc_ref[...] + _sigmoid(gl) * proj


def _kv_kernel(mem_ref, g_ref, wk_ref, wv_ref, k_ref, v_ref):
    mn = _mx(_rms(mem_ref[...], g_ref[...]))
    k_ref[...] = _dot(mn, wk_ref[...]).astype(k_ref.dtype)
    v_ref[...] = _dot(mn, wv_ref[...]).astype(v_ref.dtype)


def _tail_kernel(n_seq, x_ref, gpre_ref, wa_ref, wgl_ref, wg_in_ref, dww_ref, dwb_ref, lng_ref,
                 lnb_ref, wbr_ref, acc_ref, wout_ref, gpost_ref,
                 gxa_ref, wq_ref, k_ref, v_ref, wo_ref, gxa_post_ref,
                 gmlp_ref, w1_ref, w2_ref, gmlp_post_ref,
                 out_ref, xmix_ref, ext_ref, sh_ref, y_ref, o_ref):
    ts, d = x_ref.shape
    g = pl.program_id(0)
    nc = d // CONF_COLS
    n_sh = ts + CONF_PAD - SUBLANES
    hd_dim = d // XA_HEADS
    fc = w1_ref.shape[1] // nc

    @pl.when(g == 0)
    def _():
        xmix_ref[...] = jnp.zeros_like(xmix_ref)

    @pl.when(lax.rem(g, n_seq) == 0)
    def _():
        ext_ref[0:CONF_PAD, :] = jnp.zeros((CONF_PAD, d), F32)

    x = x_ref[...]
    h = _mx(_rms(x, gpre_ref[...]))

    def project(c):
        cols = slice(c * CONF_COLS, (c + 1) * CONF_COLS)
        return _dot(h, wa_ref[:, cols]), _dot(h, wgl_ref[:, cols])

    cur = project(0)
    gl = _dot(h, wg_in_ref[...])

    xm = xmix_ref[...]
    hb = _mx(_rms(xm, gxa_ref[...]))
    q = _mx(_dot(hb, wq_ref[...]) * (float(hd_dim) ** -0.5))
    for hd in range(XA_HEADS):
        c0 = hd * hd_dim
        sc = _dot_nt(q[:, c0:c0 + hd_dim], k_ref[:, c0:c0 + hd_dim])
        pr = jnp.exp(sc - jnp.max(sc, axis=-1, keepdims=True))
        denom = jnp.sum(pr, axis=-1, keepdims=True)
        o_ref[:, c0:c0 + hd_dim] = _dot(_mx(pr), v_ref[:, c0:c0 + hd_dim]) / denom
    x2 = xm + _rms(_dot(_mx(o_ref[...]), wo_ref[...]), gxa_post_ref[...])
    h2 = _mx(_rms(x2, gmlp_ref[...]))

    ffacc = None
    for c in range(nc):
        cols = slice(c * CONF_COLS, (c + 1) * CONF_COLS)
        nxt = project(c + 1) if c + 1 < nc else None
        a = jnp.maximum(_dot(h2, w1_ref[:, c * fc:(c + 1) * fc]), 0.0)
        part = _dot(_mx(a * a), w2_ref[c * fc:(c + 1) * fc, :])
        ffacc = part if ffacc is None else ffacc + part

        va, vg = cur
        u = va * _sigmoid(vg)
        ext_ref[CONF_PAD:CONF_PAD + ts, cols] = u
        for r in range(1, SUBLANES):
            sh_ref[r - 1] = ext_ref[pl.ds(SUBLANES - r, n_sh), cols]
        y = dwb_ref[:, cols] + dww_ref[CONF_WIDTH - 1:CONF_WIDTH, cols] * u
        for kk in range(CONF_WIDTH - 1):
            off = CONF_PAD - (CONF_WIDTH - 1) + kk
            r = (-off) % SUBLANES
            if r == 0:
                tap = ext_ref[pl.ds(off, ts), cols]
            else:
                tap = sh_ref[r - 1, pl.ds(off - (SUBLANES - r), ts), :]
            y = y + dww_ref[kk:kk + 1, cols] * tap
        ext_ref[0:CONF_PAD, cols] = ext_ref[ts:ts + CONF_PAD, cols]
        y_ref[:, cols] = y
        cur = nxt
    out_ref[...] = x2 + _rms(ffacc, gmlp_post_ref[...])

    y = y_ref[...]
    mu = jnp.mean(y, axis=-1, keepdims=True)
    yc = y - mu
    var = jnp.mean(yc * yc, axis=-1, keepdims=True)
    yn = yc * lax.rsqrt(var + EPS) * lng_ref[...] + lnb_ref[...]
    proj = _dot(_mx(yn * _sigmoid(yn)), wbr_ref[...])
    mixed = acc_ref[...] + _sigmoid(gl) * proj
    xmix_ref[...] = x + _rms(_dot(_mx(mixed), wout_ref[...]), gpost_ref[...])


def _params(n_axes):
    return pltpu.CompilerParams(dimension_semantics=("arbitrary",) * n_axes,
                                vmem_limit_bytes=VMEM_LIMIT_BYTES)


def _resident(block_shape, index_map):
    return pl.BlockSpec(block_shape, index_map, pipeline_mode=pl.Buffered(1))


def _seq_tile(s, want):
    t = min(want, s)
    assert s % t == 0 and t % SUBLANES == 0
    return t


def _layer(x, mem, layer, p):
    b, s, d = x.shape
    m = mem.shape[1]
    ts = _seq_tile(s, BRANCH_TILE)
    chunk = min(HGRN_CHUNK, ts)
    assert ts % chunk == 0 and chunk >= 2 * SUBLANES and d % (HGRN_HEADS * LANES) == 0
    assert d % CONF_COLS == 0
    dk = d // HGRN_HEADS
    grid = (b, s // ts)
    tok = pl.BlockSpec((None, ts, d), lambda bi, si: (bi, si, 0))
    tok_shape = jax.ShapeDtypeStruct((b, s, d), F32)

    def lay(arr):
        return _resident((None,) + arr.shape[1:], lambda *g: (layer,) + (0,) * (arr.ndim - 1))

    def win(col):
        return _resident((None, d, d), lambda *g: (layer, 0, col))

    def wbr(n):
        return _resident((None, None, d, d), lambda *g: (layer, n, 0, 0))

    wsq = _resident((None, d, d), lambda *g: (layer, 0, 0))

    xf = x.reshape(b * s, d)

    def skewed(t, n):
        return (pl.BlockSpec((t, d), lambda g: (jnp.minimum(g, n - 1), 0)),
                pl.BlockSpec((t, d), lambda g: (jnp.maximum(g - 1, 0), 0)))

    n_lru = b * (s // ts)
    cur_tile, prev_tile = skewed(ts, n_lru)
    acc = pl.pallas_call(
        functools.partial(_lru_kernel, s // ts), grid=(n_lru + 1,),
        out_shape=jax.ShapeDtypeStruct((b * s, d), F32),
        in_specs=[cur_tile, lay(p["g_pre_mix"]), win(COL_LRU_X), win(COL_LRU_Y),
                  win(COL_GATE0 + 0),
                  lay(p["lru_conv_w"]), lay(p["lru_conv_b"]), lay(p["lru_wx"]), lay(p["lru_bx"]),
                  lay(p["lru_wa"]), lay(p["lru_ba"]), lay(p["lru_lambda"]), wbr(0)],
        out_specs=prev_tile,
        scratch_shapes=[pltpu.VMEM((3, ts, d), F32), pltpu.VMEM((ts + LRU_PAD, d), F32),
                        pltpu.VMEM((1, d), F32)],
        compiler_params=_params(1), name="mix_lru",
    )(xf, p["g_pre_mix"], p["w_in"], p["w_in"], p["w_in"], p["lru_conv_w"], p["lru_conv_b"],
      p["lru_wx"], p["lru_bx"], p["lru_wa"], p["lru_ba"], p["lru_lambda"], p["w_branch"])
    acc = acc.reshape(b, s, d)

    acc = pl.pallas_call(
        functools.partial(_hgrn_kernel, layer), grid=grid, out_shape=tok_shape,
        in_specs=[tok, lay(p["g_pre_mix"]), win(COL_HQ), win(COL_HF), win(COL_HI), win(COL_HG),
                  win(COL_GATE0 + 1),
                  _resident(p["hgrn_lb"].shape, lambda bi, si: (0, 0)),
                  lay(p["hgrn_norm_g"]), wbr(1), tok],
        out_specs=tok,
        scratch_shapes=[pltpu.VMEM((HGRN_HEADS, dk, dk), F32), pltpu.VMEM((ts, d), F32),
                        pltpu.VMEM((chunk, d), F32), pltpu.VMEM((chunk, d), F32)],
        compiler_params=_params(2), name="mix_hgrn",
    )(x, p["g_pre_mix"], p["w_in"], p["w_in"], p["w_in"], p["w_in"], p["w_in"], p["hgrn_lb"],
      p["hgrn_norm_g"], p["w_branch"], acc)

    mem_spec = pl.BlockSpec((None, m, d), lambda bi: (bi, 0, 0))
    k, v = pl.pallas_call(
        _kv_kernel, grid=(b,),
        out_shape=[jax.ShapeDtypeStruct((b, m, d), MXU_DTYPE)] * 2,
        in_specs=[mem_spec, lay(p["g_mem"]), wsq, _resident((None, d, d), lambda bi: (layer, 0, 1))],
        out_specs=[mem_spec, mem_spec],
        compiler_params=_params(1), name="xa_kv",
    )(mem, p["g_mem"], p["xa_wkv"], p["xa_wkv"])

    ts = _seq_tile(s, TAIL_TILE)
    n_seq = s // ts
    n_tiles = b * n_seq
    accf = acc.reshape(b * s, d)
    cur_tile, prev_tile = skewed(ts, n_tiles)
    kv_spec = pl.BlockSpec((None, m, d), lambda g: (jnp.maximum(g - 1, 0) // n_seq, 0, 0))
    out = pl.pallas_call(
        functools.partial(_tail_kernel, n_seq), grid=(n_tiles + 1,),
        out_shape=jax.ShapeDtypeStruct((b * s, d), F32),
        in_specs=[cur_tile, lay(p["g_pre_mix"]), win(COL_CA), win(COL_CG), win(COL_GATE0 + 2),
                  lay(p["conf_dw_w"]), lay(p["conf_dw_b"]), lay(p["conf_ln_g"]), lay(p["conf_ln_b"]),
                  wbr(2), cur_tile, wsq, lay(p["g_post_mix"]),
                  lay(p["g_pre_xa"]), wsq, kv_spec, kv_spec, wsq, lay(p["g_post_xa"]),
                  lay(p["g_pre_mlp"]), lay(p["mlp_w1"]), lay(p["mlp_w2"]), lay(p["g_post_mlp"])],
        out_specs=prev_tile,
        scratch_shapes=[pltpu.VMEM((ts, d), F32), pltpu.VMEM((ts + CONF_PAD, d), F32),
                        pltpu.VMEM((SUBLANES - 1, ts + CONF_PAD - SUBLANES, CONF_COLS), F32),
                        pltpu.VMEM((ts, d), F32), pltpu.VMEM((ts, d), F32)],
        compiler_params=_params(1), name="tail",
    )(xf, p["g_pre_mix"], p["w_in"], p["w_in"], p["w_in"], p["conf_dw_w"], p["conf_dw_b"],
      p["conf_ln_g"], p["conf_ln_b"], p["w_branch"], accf, p["w_mix_out"], p["g_post_mix"],
      p["g_pre_xa"], p["xa_wq"], k, v, p["xa_wo"], p["g_post_xa"],
      p["g_pre_mlp"], p["mlp_w1"], p["mlp_w2"], p["g_post_mlp"])
    return out.reshape(b, s, d)


def kernel(x, mem, hgrn_lower_bounds, norm_pre_mix, norm_post_mix, w_in, lru_conv_w, lru_conv_b, lru_wx, lru_bx, lru_wa, lru_ba, lru_lambda, hgrn_norm_g, conf_dw_w, conf_dw_b, conf_ln_g, conf_ln_b, w_branch, w_mix_out, norm_pre_xa, norm_post_xa, mem_norm_g, xa_wq, xa_wkv, xa_wo, norm_pre_mlp, norm_post_mlp, mlp_w1, mlp_w2):
    depth = w_in.shape[0]
    row = lambda a: a[:, None, :].astype(F32)
    p = dict(
        g_pre_mix=row(norm_pre_mix), g_post_mix=row(norm_post_mix), w_in=_mx(w_in),
        lru_conv_w=lru_conv_w.astype(F32), lru_conv_b=row(lru_conv_b),
        lru_wx=_mx(lru_wx), lru_bx=row(lru_bx), lru_wa=_mx(lru_wa), lru_ba=row(lru_ba),
        lru_lambda=row(lru_lambda), hgrn_lb=hgrn_lower_bounds.astype(F32),
        hgrn_norm_g=row(hgrn_norm_g), conf_dw_w=conf_dw_w.astype(F32), conf_dw_b=row(conf_dw_b),
        conf_ln_g=row(conf_ln_g), conf_ln_b=row(conf_ln_b), w_branch=_mx(w_branch),
        w_mix_out=_mx(w_mix_out), g_pre_xa=row(norm_pre_xa), g_post_xa=row(norm_post_xa),
        g_mem=row(mem_norm_g), xa_wq=_mx(xa_wq), xa_wkv=_mx(xa_wkv), xa_wo=_mx(xa_wo),
        g_pre_mlp=row(norm_pre_mlp), g_post_mlp=row(norm_post_mlp),
        mlp_w1=_mx(mlp_w1), mlp_w2=_mx(mlp_w2))
    for layer in range(depth):
        x = _layer(x, mem, layer, p)
    return x
```

```python
import functools

import jax
import jax.numpy as jnp
from jax import lax
from jax.experimental import pallas as pl
from jax.experimental.pallas import tpu as pltpu

F32 = jnp.float32
MXU_DTYPE = jnp.bfloat16

EPS = 1e-6
LRU_HEADS = 4
LRU_CONV_WIDTH = 4
LRU_C = 8.0
HGRN_HEADS = 8
GATE_CLIP = 30.0
CONF_WIDTH = 31
XA_HEADS = 4
LOG2E = 1.4426950408889634
CONF_COLS = 256

LANES = 128
SUBLANES = 8
VMEM_LIMIT_BYTES = 56 * 1024 * 1024

BRANCH_TILE = 512
TAIL_TILE = 256
HGRN_CHUNK = 128
CONF_PAD = 32
LRU_PAD = 8

COL_LRU_X, COL_LRU_Y, COL_HQ, COL_HF, COL_HI, COL_HG, COL_CA, COL_CG, COL_GATE0 = range(9)


def _dot(a, b):
    return jnp.dot(a, b, preferred_element_type=F32)


def _dot_nt(a, b):
    return lax.dot_general(a, b, (((1,), (1,)), ((), ())), preferred_element_type=F32)


def _dot_tn(a, b):
    return lax.dot_general(a, b, (((0,), (0,)), ((), ())), preferred_element_type=F32)


def _mx(a):
    return a.astype(MXU_DTYPE)


def _rms(x, g):
    return x * lax.rsqrt(jnp.mean(x * x, axis=-1, keepdims=True) + EPS) * g


def _sigmoid(x):
    return jax.nn.sigmoid(x)


def _softplus(y):
    return jnp.maximum(y, 0.0) + jnp.log1p(jnp.exp(-jnp.abs(y)))


def _gelu_tanh(x):
    c = 0.7978845608028654
    return 0.5 * x * (1.0 + jnp.tanh(c * (x + 0.044715 * (x * x * x))))


def _lru_kernel(n_seq, x_ref, gpre_ref, wx_in_ref, wy_in_ref, wg_in_ref, cw_ref, cb_ref,
                wgx_ref, bgx_ref, wga_ref, bga_ref, lam_ref, wbr_ref,
                out_ref, z_ref, ext_ref, h_ref):
    ts, d = x_ref.shape
    blk = d // LRU_HEADS
    g = pl.program_id(0)

    @pl.when(g == 0)
    def _():
        z_ref[...] = jnp.zeros_like(z_ref)

    @pl.when(jnp.logical_or(g == 0, lax.rem(g + (n_seq - 1), n_seq) == 0))
    def _():
        ext_ref[0:LRU_PAD, :] = jnp.zeros((LRU_PAD, d), F32)
        h_ref[...] = jnp.zeros_like(h_ref)

    xb = z_ref[0]
    yb = z_ref[1]
    gl = z_ref[2]

    ext_ref[LRU_PAD:LRU_PAD + ts, :] = xb
    xc = cb_ref[...] + cw_ref[LRU_CONV_WIDTH - 1:LRU_CONV_WIDTH, :] * xb
    for k in range(LRU_CONV_WIDTH - 1):
        off = LRU_PAD - (LRU_CONV_WIDTH - 1) + k
        xc = xc + cw_ref[k:k + 1, :] * ext_ref[pl.ds(off, ts), :]
    ext_ref[0:LRU_PAD, :] = ext_ref[ts:ts + LRU_PAD, :]

    xcm = _mx(xc)
    gx = jnp.concatenate(
        [_dot(xcm[:, i * blk:(i + 1) * blk], wgx_ref[i]) for i in range(LRU_HEADS)], axis=1)
    ga = jnp.concatenate(
        [_dot(xcm[:, i * blk:(i + 1) * blk], wga_ref[i]) for i in range(LRU_HEADS)], axis=1)

    hn = _mx(_rms(x_ref[...], gpre_ref[...]))
    zx = _dot(hn, wx_in_ref[...])
    zy = _dot(hn, wy_in_ref[...])
    zg = _dot(hn, wg_in_ref[...])

    gate_x = _sigmoid(gx + bgx_ref[...])
    gate_a = _sigmoid(ga + bga_ref[...])

    log_a = jnp.minimum(-LRU_C * gate_a * _softplus(-lam_ref[...]), -1e-6)
    a = jnp.exp(log_a)
    m = -jnp.tanh(log_a) * (a * a + 1.0)
    u = xc * gate_x * (m * lax.rsqrt(m))

    row = lax.broadcasted_iota(jnp.int32, (ts, d), 0)
    sh = 1
    while sh < ts:
        valid = row >= sh
        a_sh = pltpu.roll(a, sh, axis=0)
        u_sh = pltpu.roll(u, sh, axis=0)
        u = jnp.where(valid, a * u_sh + u, u)
        a = jnp.where(valid, a * a_sh, a)
        sh *= 2
    hs = u + a * h_ref[...]
    h_ref[...] = hs[ts - 1:ts, :]

    y = hs * _gelu_tanh(yb)
    proj = _dot(_mx(y), wbr_ref[...])
    out_ref[...] = _sigmoid(gl) * proj

    z_ref[0] = zx
    z_ref[1] = zy
    z_ref[2] = zg


def _hgrn_constants(c, d):
    row = lax.broadcasted_iota(jnp.int32, (c, d), 0)
    sub = row & (SUBLANES - 1)
    pair_xor = (lax.broadcasted_iota(jnp.int32, (c, c), 0)
                ^ lax.broadcasted_iota(jnp.int32, (c, c), 1))
    cst = dict(scan=[sub >= sh for sh in (1, 2, 4)], lo4=sub < 4, odd=(row & 1) == 1,
               second={s: (row & (s - 1)) >= s // 2 for s in (4, 8)},
               same_block={}, sign={})
    s = c // 2
    while s >= 2:
        cst["same_block"][s] = pair_xor < s
        s //= 2
    for s in (4, 8):
        cst["sign"][s] = jnp.where(cst["second"][s], 1.0, -1.0).astype(F32)
    return cst


def _hgrn_chunk(q, k, v, gl, st_ref, cst, c_ref, b_ref, heads):
    c, d = q.shape
    dk = d // heads
    ngrp = c // SUBLANES
    hcols = [slice(hd * dk, (hd + 1) * dk) for hd in range(heads)]

    cs = gl
    for sh, m in zip((1, 2, 4), cst["scan"]):
        cs = cs + jnp.where(m, pltpu.roll(cs, sh, axis=0), 0.0)
    c_ref[...] = cs
    off = jnp.zeros((1, d), F32)
    parts = []
    for g in range(ngrp):
        parts.append((cs[g * SUBLANES:(g + 1) * SUBLANES, :] + off) * LOG2E)
        off = off + c_ref[g * SUBLANES + SUBLANES - 1:(g + 1) * SUBLANES, :]
    b2 = jnp.concatenate(parts, axis=0)
    b_ref[...] = b2
    b2_last = b_ref[c - 1:c, :]

    def grp(arr, g0, n):
        return arr[g0 * SUBLANES:(g0 + n) * SUBLANES, :]

    def ref_row(r):
        return b_ref[r:r + 1, :]

    attn = [None] * heads

    def add_level(qs, ks, s):
        qm, km = _mx(qs), _mx(ks)
        for hd in range(heads):
            a = _dot_nt(qm[:, hcols[hd]], km[:, hcols[hd]])
            attn[hd] = a if attn[hd] is None else jnp.where(cst["same_block"][s], a, attn[hd])

    s = c
    while s >= 2 * SUBLANES:
        half = s // 2
        hg = half // SUBLANES
        qs, ks = [], []
        for blk in range(c // s):
            g0 = blk * (s // SUBLANES)
            rr = ref_row(blk * s + half - 1)
            ks += [grp(k, g0, hg) * jnp.exp2(rr - grp(b2, g0, hg)), jnp.zeros((half, d), F32)]
            qs += [jnp.zeros((half, d), F32),
                   grp(q, g0 + hg, hg) * jnp.exp2(grp(b2, g0 + hg, hg) - rr)]
        add_level(jnp.concatenate(qs, axis=0), jnp.concatenate(ks, axis=0), s)
        s = half
    for s in (8, 4):
        if s == 8:
            rows = jnp.concatenate([jnp.broadcast_to(ref_row(g * SUBLANES + 3), (SUBLANES, d))
                                    for g in range(ngrp)], axis=0)
        else:
            lo = jnp.concatenate([jnp.broadcast_to(ref_row(g * SUBLANES + 1), (SUBLANES, d))
                                  for g in range(ngrp)], axis=0)
            hi = jnp.concatenate([jnp.broadcast_to(ref_row(g * SUBLANES + 5), (SUBLANES, d))
                                  for g in range(ngrp)], axis=0)
            rows = jnp.where(cst["lo4"], lo, hi)
        decay = jnp.exp2((b2 - rows) * cst["sign"][s])
        second = cst["second"][s]
        add_level(jnp.where(second, q * decay, 0.0), jnp.where(second, 0.0, k * decay), s)
    odd = cst["odd"]
    add_level(jnp.where(odd, q * jnp.exp2(gl * LOG2E), 0.0), jnp.where(odd, 0.0, k), 2)

    qk = q * k
    qe = _mx(q * jnp.exp2(b2))
    ke = _mx(k * jnp.exp2(b2_last - b2))
    vm = _mx(v)
    dec = jnp.exp2(b2_last)
    outs = []
    for hd in range(heads):
        hc = hcols[hd]
        st = st_ref[hd]
        diag = jnp.sum(qk[:, hc], axis=-1, keepdims=True)
        outs.append(_dot_nt(qe[:, hc], _mx(st)) + _dot(_mx(attn[hd]), vm[:, hc]) + diag * v[:, hc])
        st_ref[hd] = st * dec[:, hc] + _dot_tn(vm[:, hc], ke[:, hc])
    return outs


def _hgrn_kernel(layer, x_ref, gpre_ref, wq_ref, wf_ref, wi_ref, wgo_ref, wg_in_ref, lbraw_ref,
                 ng_ref, wbr_ref, acc_ref, out_ref, st_ref, y_ref, c_ref, b_ref):
    ts, d = x_ref.shape
    dk = d // HGRN_HEADS
    chunk = min(HGRN_CHUNK, ts)

    @pl.when(pl.program_id(1) == 0)
    def _():
        st_ref[...] = jnp.zeros_like(st_ref)

    n_layers = lbraw_ref.shape[0]
    rows = [lbraw_ref[r:r + 1, :] for r in range(n_layers)]
    mx = functools.reduce(jnp.maximum, rows)
    es = [jnp.exp(r - mx) for r in rows]
    tot = functools.reduce(lambda p, t: p + t, es)
    soft = [e / tot for e in es]
    lb = functools.reduce(lambda p, t: p + t, soft[:layer + 1]) - soft[0]

    x = x_ref[...]
    h = _mx(_rms(x, gpre_ref[...]))
    q = _dot(h, wq_ref[...])
    f = _dot(h, wf_ref[...])
    v = _dot(h, wi_ref[...])
    go = _dot(h, wgo_ref[...])
    gl = _dot(h, wg_in_ref[...])

    qf = q * _sigmoid(q) * (float(dk) ** -0.5)
    sg = _sigmoid(jnp.clip(f, -GATE_CLIP, GATE_CLIP))
    logf = jnp.minimum(jnp.log(lb + (1.0 - lb) * sg), 0.0)
    kf = jnp.maximum((1.0 - lb) * (1.0 - sg), 0.0)
    gg = go * _sigmoid(go)
    cst = _hgrn_constants(chunk, d)

    for ci in range(ts // chunk):
        r0 = ci * chunk
        outs = _hgrn_chunk(qf[r0:r0 + chunk, :], kf[r0:r0 + chunk, :], v[r0:r0 + chunk, :],
                           logf[r0:r0 + chunk, :], st_ref, cst, c_ref, b_ref, HGRN_HEADS)
        for hd, o in enumerate(outs):
            c0 = hd * dk
            on = o * lax.rsqrt(jnp.mean(o * o, axis=-1, keepdims=True) + EPS) * ng_ref[...]
            y_ref[r0:r0 + chunk, c0:c0 + dk] = on * gg[r0:r0 + chunk, c0:c0 + dk]

    proj = _dot(_mx(y_ref[...]), wbr_ref[...])
    out_ref[...] = acc_ref[...] + _sigmoid(gl) * proj


def _kv_kernel(mem_ref, g_ref, wk_ref, wv_ref, k_ref, v_ref):
    mn = _mx(_rms(mem_ref[...], g_ref[...]))
    k_ref[...] = _dot(mn, wk_ref[...]).astype(k_ref.dtype)
    v_ref[...] = _dot(mn, wv_ref[...]).astype(v_ref.dtype)


def _tail_kernel(n_seq, x_ref, gpre_ref, wa_ref, wgl_ref, wg_in_ref, dww_ref, dwb_ref, lng_ref,
                 lnb_ref, wbr_ref, acc_ref, wout_ref, gpost_ref,
                 gxa_ref, wq_ref, k_ref, v_ref, wo_ref, gxa_post_ref,
                 gmlp_ref, w1_ref, w2_ref, gmlp_post_ref,
                 out_ref, xmix_ref, ext_ref, sh_ref, y_ref, o_ref):
    ts, d = x_ref.shape
    g = pl.program_id(0)
    nc = d // CONF_COLS
    n_sh = ts + CONF_PAD - SUBLANES
    hd_dim = d // XA_HEADS
    fc = w1_ref.shape[1] // nc

    @pl.when(g == 0)
    def _():
        xmix_ref[...] = jnp.zeros_like(xmix_ref)

    @pl.when(lax.rem(g, n_seq) == 0)
    def _():
        ext_ref[0:CONF_PAD, :] = jnp.zeros((CONF_PAD, d), F32)

    x = x_ref[...]
    h = _mx(_rms(x, gpre_ref[...]))

    def project(c):
        cols = slice(c * CONF_COLS, (c + 1) * CONF_COLS)
        return _dot(h, wa_ref[:, cols]), _dot(h, wgl_ref[:, cols])

    cur = project(0)
    gl = _dot(h, wg_in_ref[...])

    xm = xmix_ref[...]
    hb = _mx(_rms(xm, gxa_ref[...]))
    q = _mx(_dot(hb, wq_ref[...]) * (float(hd_dim) ** -0.5))
    for hd in range(XA_HEADS):
        c0 = hd * hd_dim
        sc = _dot_nt(q[:, c0:c0 + hd_dim], k_ref[:, c0:c0 + hd_dim])
        pr = jnp.exp(sc - jnp.max(sc, axis=-1, keepdims=True))
        denom = jnp.sum(pr, axis=-1, keepdims=True)
        o_ref[:, c0:c0 + hd_dim] = _dot(_mx(pr), v_ref[:, c0:c0 + hd_dim]) / denom
    x2 = xm + _rms(_dot(_mx(o_ref[...]), wo_ref[...]), gxa_post_ref[...])
    h2 = _mx(_rms(x2, gmlp_ref[...]))

    ffacc = None
    for c in range(nc):
        cols = slice(c * CONF_COLS, (c + 1) * CONF_COLS)
        nxt = project(c + 1) if c + 1 < nc else None
        a = jnp.maximum(_dot(h2, w1_ref[:, c * fc:(c + 1) * fc]), 0.0)
        part = _dot(_mx(a * a), w2_ref[c * fc:(c + 1) * fc, :])
        ffacc = part if ffacc is None else ffacc + part

        va, vg = cur
        u = va * _sigmoid(vg)
        if c > 0:
            u = u + pin
        pz = part[0:SUBLANES, 0:LANES]
        pz = jnp.where(pz == pz, jnp.clip(pz, -1.0, 1.0), 0.0) * 0.0
        pin = jnp.tile(pz, (ts // SUBLANES, CONF_COLS // LANES))
        ext_ref[CONF_PAD:CONF_PAD + ts, cols] = u
        for r in range(1, SUBLANES):
            sh_ref[r - 1] = ext_ref[pl.ds(SUBLANES - r, n_sh), cols]
        y = dwb_ref[:, cols] + dww_ref[CONF_WIDTH - 1:CONF_WIDTH, cols] * u
        for kk in range(CONF_WIDTH - 1):
            off = CONF_PAD - (CONF_WIDTH - 1) + kk
            r = (-off) % SUBLANES
            if r == 0:
                tap = ext_ref[pl.ds(off, ts), cols]
            else:
                tap = sh_ref[r - 1, pl.ds(off - (SUBLANES - r), ts), :]
            y = y + dww_ref[kk:kk + 1, cols] * tap
        ext_ref[0:CONF_PAD, cols] = ext_ref[ts:ts + CONF_PAD, cols]
        y_ref[:, cols] = y
        cur = nxt
    out_ref[...] = x2 + _rms(ffacc, gmlp_post_ref[...])

    y = y_ref[...]
    mu = jnp.mean(y, axis=-1, keepdims=True)
    yc = y - mu
    var = jnp.mean(yc * yc, axis=-1, keepdims=True)
    yn = yc * lax.rsqrt(var + EPS) * lng_ref[...] + lnb_ref[...]
    proj = _dot(_mx(yn * _sigmoid(yn)), wbr_ref[...])
    mixed = acc_ref[...] + _sigmoid(gl) * proj
    xmix_ref[...] = x + _rms(_dot(_mx(mixed), wout_ref[...]), gpost_ref[...])


def _params(n_axes):
    return pltpu.CompilerParams(dimension_semantics=("arbitrary",) * n_axes,
                                vmem_limit_bytes=VMEM_LIMIT_BYTES)


def _resident(block_shape, index_map):
    return pl.BlockSpec(block_shape, index_map, pipeline_mode=pl.Buffered(1))


def _seq_tile(s, want):
    t = min(want, s)
    assert s % t == 0 and t % SUBLANES == 0
    return t


def _layer(x, mem, layer, p):
    b, s, d = x.shape
    m = mem.shape[1]
    ts = _seq_tile(s, BRANCH_TILE)
    chunk = min(HGRN_CHUNK, ts)
    assert ts % chunk == 0 and chunk >= 2 * SUBLANES and d % (HGRN_HEADS * LANES) == 0
    assert d % CONF_COLS == 0
    dk = d // HGRN_HEADS
    grid = (b, s // ts)
    tok = pl.BlockSpec((None, ts, d), lambda bi, si: (bi, si, 0))
    tok_shape = jax.ShapeDtypeStruct((b, s, d), F32)

    def lay(arr):
        return _resident((None,) + arr.shape[1:], lambda *g: (layer,) + (0,) * (arr.ndim - 1))

    def win(col):
        return _resident((None, d, d), lambda *g: (layer, 0, col))

    def wbr(n):
        return _resident((None, None, d, d), lambda *g: (layer, n, 0, 0))

    wsq = _resident((None, d, d), lambda *g: (layer, 0, 0))

    xf = x.reshape(b * s, d)

    def skewed(t, n):
        return (pl.BlockSpec((t, d), lambda g: (jnp.minimum(g, n - 1), 0)),
                pl.BlockSpec((t, d), lambda g: (jnp.maximum(g - 1, 0), 0)))

    n_lru = b * (s // ts)
    cur_tile, prev_tile = skewed(ts, n_lru)
    acc = pl.pallas_call(
        functools.partial(_lru_kernel, s // ts), grid=(n_lru + 1,),
        out_shape=jax.ShapeDtypeStruct((b * s, d), F32),
        in_specs=[cur_tile, lay(p["g_pre_mix"]), win(COL_LRU_X), win(COL_LRU_Y),
                  win(COL_GATE0 + 0),
                  lay(p["lru_conv_w"]), lay(p["lru_conv_b"]), lay(p["lru_wx"]), lay(p["lru_bx"]),
                  lay(p["lru_wa"]), lay(p["lru_ba"]), lay(p["lru_lambda"]), wbr(0)],
        out_specs=prev_tile,
        scratch_shapes=[pltpu.VMEM((3, ts, d), F32), pltpu.VMEM((ts + LRU_PAD, d), F32),
                        pltpu.VMEM((1, d), F32)],
        compiler_params=_params(1), name="mix_lru",
    )(xf, p["g_pre_mix"], p["w_in"], p["w_in"], p["w_in"], p["lru_conv_w"], p["lru_conv_b"],
      p["lru_wx"], p["lru_bx"], p["lru_wa"], p["lru_ba"], p["lru_lambda"], p["w_branch"])
    acc = acc.reshape(b, s, d)

    acc = pl.pallas_call(
        functools.partial(_hgrn_kernel, layer), grid=grid, out_shape=tok_shape,
        in_specs=[tok, lay(p["g_pre_mix"]), win(COL_HQ), win(COL_HF), win(COL_HI), win(COL_HG),
                  win(COL_GATE0 + 1),
                  _resident(p["hgrn_lb"].shape, lambda bi, si: (0, 0)),
                  lay(p["hgrn_norm_g"]), wbr(1), tok],
        out_specs=tok,
        scratch_shapes=[pltpu.VMEM((HGRN_HEADS, dk, dk), F32), pltpu.VMEM((ts, d), F32),
                        pltpu.VMEM((chunk, d), F32), pltpu.VMEM((chunk, d), F32)],
        compiler_params=_params(2), name="mix_hgrn",
    )(x, p["g_pre_mix"], p["w_in"], p["w_in"], p["w_in"], p["w_in"], p["w_in"], p["hgrn_lb"],
      p["hgrn_norm_g"], p["w_branch"], acc)

    mem_spec = pl.BlockSpec((None, m, d), lambda bi: (bi, 0, 0))
    k, v = pl.pallas_call(
        _kv_kernel, grid=(b,),
        out_shape=[jax.ShapeDtypeStruct((b, m, d), MXU_DTYPE)] * 2,
        in_specs=[mem_spec, lay(p["g_mem"]), wsq, _resident((None, d, d), lambda bi: (layer, 0, 1))],
        out_specs=[mem_spec, mem_spec],
        compiler_params=_params(1), name="xa_kv",
    )(mem, p["g_mem"], p["xa_wkv"], p["xa_wkv"])

    ts = _seq_tile(s, TAIL_TILE)
    n_seq = s // ts
    n_tiles = b * n_seq
    accf = acc.reshape(b * s, d)
    cur_tile, prev_tile = skewed(ts, n_tiles)
    kv_spec = pl.BlockSpec((None, m, d), lambda g: (jnp.maximum(g - 1, 0) // n_seq, 0, 0))
    out = pl.pallas_call(
        functools.partial(_tail_kernel, n_seq), grid=(n_tiles + 1,),
        out_shape=jax.ShapeDtypeStruct((b * s, d), F32),
        in_specs=[cur_tile, lay(p["g_pre_mix"]), win(COL_CA), win(COL_CG), win(COL_GATE0 + 2),
                  lay(p["conf_dw_w"]), lay(p["conf_dw_b"]), lay(p["conf_ln_g"]), lay(p["conf_ln_b"]),
                  wbr(2), cur_tile, wsq, lay(p["g_post_mix"]),
                  lay(p["g_pre_xa"]), wsq, kv_spec, kv_spec, wsq, lay(p["g_post_xa"]),
                  lay(p["g_pre_mlp"]), lay(p["mlp_w1"]), lay(p["mlp_w2"]), lay(p["g_post_mlp"])],
        out_specs=prev_tile,
        scratch_shapes=[pltpu.VMEM((ts, d), F32), pltpu.VMEM((ts + CONF_PAD, d), F32),
                        pltpu.VMEM((SUBLANES - 1, ts + CONF_PAD - SUBLANES, CONF_COLS), F32),
                        pltpu.VMEM((ts, d), F32), pltpu.VMEM((ts, d), F32)],
        compiler_params=_params(1), name="tail",
    )(xf, p["g_pre_mix"], p["w_in"], p["w_in"], p["w_in"], p["conf_dw_w"], p["conf_dw_b"],
      p["conf_ln_g"], p["conf_ln_b"], p["w_branch"], accf, p["w_mix_out"], p["g_post_mix"],
      p["g_pre_xa"], p["xa_wq"], k, v, p["xa_wo"], p["g_post_xa"],
      p["g_pre_mlp"], p["mlp_w1"], p["mlp_w2"], p["g_post_mlp"])
    return out.reshape(b, s, d)


def kernel(x, mem, hgrn_lower_bounds, norm_pre_mix, norm_post_mix, w_in, lru_conv_w, lru_conv_b, lru_wx, lru_bx, lru_wa, lru_ba, lru_lambda, hgrn_norm_g, conf_dw_w, conf_dw_b, conf_ln_g, conf_ln_b, w_branch, w_mix_out, norm_pre_xa, norm_post_xa, mem_norm_g, xa_wq, xa_wkv, xa_wo, norm_pre_mlp, norm_post_mlp, mlp_w1, mlp_w2):
    depth = w_in.shape[0]
    row = lambda a: a[:, None, :].astype(F32)
    p = dict(
        g_pre_mix=row(norm_pre_mix), g_post_mix=row(norm_post_mix), w_in=_mx(w_in),
        lru_conv_w=lru_conv_w.astype(F32), lru_conv_b=row(lru_conv_b),
        lru_wx=_mx(lru_wx), lru_bx=row(lru_bx), lru_wa=_mx(lru_wa), lru_ba=row(lru_ba),
        lru_lambda=row(lru_lambda), hgrn_lb=hgrn_lower_bounds.astype(F32),
        hgrn_norm_g=row(hgrn_norm_g), conf_dw_w=conf_dw_w.astype(F32), conf_dw_b=row(conf_dw_b),
        conf_ln_g=row(conf_ln_g), conf_ln_b=row(conf_ln_b), w_branch=_mx(w_branch),
        w_mix_out=_mx(w_mix_out), g_pre_xa=row(norm_pre_xa), g_post_xa=row(norm_post_xa),
        g_mem=row(mem_norm_g), xa_wq=_mx(xa_wq), xa_wkv=_mx(xa_wkv), xa_wo=_mx(xa_wo),
        g_pre_mlp=row(norm_pre_mlp), g_post_mlp=row(norm_post_mlp),
        mlp_w1=_mx(mlp_w1), mlp_w2=_mx(mlp_w2))
    for layer in range(depth):
        x = _layer(x, mem, layer, p)
    return x
```
